```python
import math
import jax
import jax.numpy as jnp
from jax import lax
import numpy as np

D_MODEL = 2048
BATCH = 8
SEQ = 4096
DEPTH = 4

GRID_W = 64
CTX_LEN = 256
HEAD_DIM = 128
NA_HEADS = 6
NA_WIDTH = NA_HEADS * HEAD_DIM
NA_WIN_R = 8
NA_WIN_C = 16
POOL_WINDOWS = (2, 4, 8, 16)
POOL_GROUP = 128
POOL_WIDTH = POOL_GROUP * len(POOL_WINDOWS)
GDN_HEADS = 6
GDN_WIDTH = GDN_HEADS * HEAD_DIM
GDN_CHUNK = 64
CONV_K = 3
MIX_WIDTH = NA_WIDTH + POOL_WIDTH + GDN_WIDTH
IN_WIDTH = 3 * NA_WIDTH + POOL_WIDTH + 4 * GDN_WIDTH + 4 * GDN_HEADS
D_FF = 5632
N_EXPERTS = 8
TOP_K = 2
D_FF_EXPERT = 4096
EPS = 1e-6

kernel_name = "hybrid_natten_pool_gdn_moe_dit"


def rms_norm(x, w=None):
    xf = x.astype(jnp.float32)
    y = xf * lax.rsqrt(jnp.mean(xf * xf, axis=-1, keepdims=True) + EPS)
    if w is not None:
        y = y * w.astype(jnp.float32)
    return y.astype(x.dtype)


def l2_normalize(x):
    xf = x.astype(jnp.float32)
    return xf * lax.rsqrt(jnp.sum(xf * xf, axis=-1, keepdims=True) + EPS)


def modulate(x, shift, scale):
    return rms_norm(x) * (1 + scale) + shift


def split_projection(p):
    sizes = (NA_WIDTH, NA_WIDTH, NA_WIDTH, POOL_WIDTH, 3 * GDN_WIDTH, GDN_WIDTH, 4 * GDN_HEADS)
    return jnp.split(p, [int(s) for s in np.cumsum(sizes)[:-1]], axis=-1)


def neighbourhood_attention(q, k, v, k_ctx, v_ctx, rpb):
    bsz, seq, heads, dh = q.shape
    rows = seq // GRID_W
    win_r = min(NA_WIN_R, rows)
    qg = q.reshape(bsz, rows, GRID_W, heads, dh)
    kg = k.reshape(bsz, rows, GRID_W, heads, dh)
    vg = v.reshape(bsz, rows, GRID_W, heads, dh)
    col = jnp.arange(GRID_W)
    c0 = jnp.clip(col - NA_WIN_C // 2, 0, GRID_W - NA_WIN_C)
    col_in = (col[None, :] >= c0[:, None]) & (col[None, :] < c0[:, None] + NA_WIN_C)
    col_idx = jnp.clip(col[None, :] - col[:, None], 1 - NA_WIN_C, NA_WIN_C - 1) + NA_WIN_C - 1
    r0 = jnp.clip(jnp.arange(rows) - win_r // 2, 0, rows - win_r)
    bias_cols = rpb[:, :, col_idx]
    n_loc = win_r * GRID_W

    def row_block(r):
        start = r0[r]
        q_r = lax.dynamic_index_in_dim(qg, r, axis=1, keepdims=False)
        k_r = lax.dynamic_slice_in_dim(kg, start, win_r, axis=1)
        v_r = lax.dynamic_slice_in_dim(vg, start, win_r, axis=1)
        row_idx = start + jnp.arange(win_r) - r + NA_WIN_R - 1
        bias = jnp.take(bias_cols, row_idx, axis=1).transpose(0, 2, 1, 3)
        s_loc = jnp.einsum('bqhd,bikhd->bhqik', q_r, k_r).astype(jnp.float32) + bias[None].astype(jnp.float32)
        s_loc = jnp.where(col_in[:, None, :], s_loc, -jnp.inf)
        s_ctx = jnp.einsum('bqhd,bchd->bhqc', q_r, k_ctx).astype(jnp.float32)
        p = jax.nn.softmax(jnp.concatenate([s_loc.reshape(bsz, heads, GRID_W, n_loc), s_ctx], axis=-1), axis=-1)
        p = p.astype(v.dtype)
        p_loc = p[..., :n_loc].reshape(bsz, heads, GRID_W, win_r, GRID_W)
        return (jnp.einsum('bhqik,bikhd->bqhd', p_loc, v_r)
                + jnp.einsum('bhqc,bchd->bqhd', p[..., n_loc:], v_ctx))

    out = lax.map(row_block, jnp.arange(rows))
    return jnp.moveaxis(out, 0, 1).reshape(bsz, seq, heads, dh)


def context_attention(q, k, v):
    s = jnp.einsum('bqhd,bkhd->bhqk', q, k).astype(jnp.float32)
    p = jax.nn.softmax(s, axis=-1).astype(v.dtype)
    return jnp.einsum('bhqk,bkhd->bqhd', p, v)


def pool_mixer(u, w_pool, scale):
    bsz, L, _ = u.shape
    uf = u.astype(jnp.float32)
    csum = jnp.concatenate([jnp.zeros((bsz, 1, POOL_WIDTH), jnp.float32), jnp.cumsum(uf, axis=1)], axis=1)
    t = jnp.arange(L)
    means = []
    for gi, win in enumerate(POOL_WINDOWS):
        lo = jnp.maximum(t - win // 2, 0)
        hi = jnp.minimum(t + win // 2, L)
        cg = csum[..., gi * POOL_GROUP:(gi + 1) * POOL_GROUP]
        means.append((cg[:, hi] - cg[:, lo]) / (hi - lo).astype(jnp.float32)[None, :, None])
    d = (jnp.concatenate(means, axis=-1) - uf).reshape(bsz, L, len(POOL_WINDOWS), POOL_GROUP)
    y = jnp.einsum('blgc,gce->blge', d, w_pool.astype(jnp.float32)).reshape(bsz, L, POOL_WIDTH)
    return (y * scale.astype(jnp.float32)).astype(u.dtype)


def short_conv(x, w):
    return lax.conv_general_dilated(x, w[:, None, :], window_strides=(1,),
                                    padding=[(CONV_K // 2, CONV_K // 2)],
                                    dimension_numbers=('NWC', 'WIO', 'NWC'),
                                    feature_group_count=x.shape[-1])


def gdn_features(p_qkv, p_ab, conv_w, a_log, dt_bias):
    bsz, L, _ = p_qkv.shape
    y = jax.nn.silu(short_conv(p_qkv, conv_w))
    q, k, v = jnp.split(y, 3, axis=-1)
    to_heads = lambda t: t.reshape(bsz, L, GDN_HEADS, HEAD_DIM).transpose(0, 2, 1, 3).astype(jnp.float32)
    q = l2_normalize(to_heads(q)) * HEAD_DIM ** -0.5
    k = l2_normalize(to_heads(k))
    v = to_heads(v)
    ab = p_ab.astype(jnp.float32).reshape(bsz, L, 2, 2, GDN_HEADS)
    g = -jnp.exp(a_log.astype(jnp.float32)) * jax.nn.softplus(ab[:, :, :, 0] + dt_bias.astype(jnp.float32))
    beta = jax.nn.sigmoid(ab[:, :, :, 1])
    return q, k, v, g.transpose(2, 0, 3, 1), beta.transpose(2, 0, 3, 1)


def gated_delta_chunks(q, k, v, g, beta, state, with_output):
    bsz, heads, L, dk = k.shape
    dv = v.shape[-1]
    cs = GDN_CHUNK
    n = L // cs
    chunk = lambda t: t.reshape(bsz, heads, n, cs, *t.shape[3:])
    q, k, v, g, beta = chunk(q), chunk(k), chunk(v), chunk(g), chunk(beta)
    gc = jnp.cumsum(g, axis=-1)
    incl = jnp.tril(jnp.ones((cs, cs), bool))
    strict = jnp.tril(jnp.ones((cs, cs), bool), -1)
    decay = jnp.where(incl, jnp.exp(jnp.where(incl, gc[..., :, None] - gc[..., None, :], 0.0)), 0.0)
    kb = k * beta[..., None]
    a_mat = jnp.where(strict, jnp.einsum('bhncd,bhnsd->bhncs', kb, k) * decay, 0.0) + jnp.eye(cs, dtype=jnp.float32)
    rhs = jnp.concatenate([v * beta[..., None], kb * jnp.exp(gc)[..., None]], axis=-1)
    sol = lax.linalg.triangular_solve(a_mat, rhs, left_side=True, lower=True, unit_diagonal=True)
    u, w = sol[..., :dv], sol[..., dv:]
    g_last = gc[..., -1]
    k_dec = k * jnp.exp(g_last[..., None] - gc)[..., None]
    mv = lambda t: jnp.moveaxis(t, 2, 0)
    if with_output:
        attn = jnp.einsum('bhncd,bhnsd->bhncs', q, k) * decay
        q_dec = q * jnp.exp(gc)[..., None]

        def step(s, xs):
            u_n, w_n, kd_n, gl_n, qd_n, at_n = xs
            v_new = u_n - jnp.einsum('bhcd,bhde->bhce', w_n, s)
            o_n = jnp.einsum('bhcd,bhde->bhce', qd_n, s) + jnp.einsum('bhcs,bhse->bhce', at_n, v_new)
            s_new = s * jnp.exp(gl_n)[..., None, None] + jnp.einsum('bhcd,bhce->bhde', kd_n, v_new)
            return s_new, o_n

        state, out = lax.scan(step, state, (mv(u), mv(w), mv(k_dec), mv(g_last), mv(q_dec), mv(attn)))
        return jnp.moveaxis(out, 0, 2).reshape(bsz, heads, L, dv), state

    def step_state(s, xs):
        u_n, w_n, kd_n, gl_n = xs
        v_new = u_n - jnp.einsum('bhcd,bhde->bhce', w_n, s)
        return s * jnp.exp(gl_n)[..., None, None] + jnp.einsum('bhcd,bhce->bhde', kd_n, v_new), None

    state, _ = lax.scan(step_state, state, (mv(u), mv(w), mv(k_dec), mv(g_last)))
    return None, state


def gdn_bidirectional(q, k, v, g, beta, state_f, state_b, with_output):
    flip = lambda t: jnp.flip(t, axis=2)
    out_f, s_f = gated_delta_chunks(q, k, v, g[0], beta[0], state_f, with_output)
    out_b, s_b = gated_delta_chunks(flip(q), flip(k), flip(v), flip(g[1]), flip(beta[1]), state_b, with_output)
    out = out_f + flip(out_b) if with_output else None
    return out, s_f, s_b


def gdn_output(o, z, norm_w):
    bsz, heads, L, dv = o.shape
    o = rms_norm(jnp.transpose(o, (0, 2, 1, 3)), norm_w).reshape(bsz, L, heads * dv)
    return (o * jax.nn.silu(z.astype(jnp.float32))).astype(z.dtype)


def token_mixers(p_ctx, p_lat, na_q_norm, na_k_norm, na_rpb, pool_w, pool_scale,
                 gdn_conv, gdn_a_log, gdn_dt_bias, gdn_norm, ctx_out):
    qc, kc, vc, uc, qkv_c, zc, ab_c = split_projection(p_ctx)
    qx, kx, vx, ux, qkv_x, zx, ab_x = split_projection(p_lat)
    bsz, seq, _ = p_lat.shape
    heads = lambda t: t.reshape(t.shape[0], t.shape[1], NA_HEADS, HEAD_DIM)
    scale = HEAD_DIM ** -0.5
    k_ctx = rms_norm(heads(kc), na_k_norm)
    v_ctx = heads(vc)
    na_x = neighbourhood_attention(rms_norm(heads(qx), na_q_norm) * scale, rms_norm(heads(kx), na_k_norm),
                                   heads(vx), k_ctx, v_ctx, na_rpb).reshape(bsz, seq, NA_WIDTH)
    pool_x = pool_mixer(ux, pool_w, pool_scale)
    qg_c, kg_c, vg_c, g_c, b_c = gdn_features(qkv_c, ab_c, gdn_conv, gdn_a_log, gdn_dt_bias)
    zero = jnp.zeros((bsz, GDN_HEADS, HEAD_DIM, HEAD_DIM), jnp.float32)
    o_c, s_f, s_b = gdn_bidirectional(qg_c, kg_c, vg_c, g_c, b_c, zero, zero, ctx_out)
    qg_x, kg_x, vg_x, g_x, b_x = gdn_features(qkv_x, ab_x, gdn_conv, gdn_a_log, gdn_dt_bias)
    o_x, _, _ = gdn_bidirectional(qg_x, kg_x, vg_x, g_x, b_x, s_f, s_b, True)
    y_lat = jnp.concatenate([na_x, pool_x, gdn_output(o_x, zx, gdn_norm)], axis=-1)
    if not ctx_out:
        return None, y_lat
    na_c = context_attention(rms_norm(heads(qc), na_q_norm) * scale, k_ctx, v_ctx).reshape(bsz, -1, NA_WIDTH)
    y_ctx = jnp.concatenate([na_c, pool_mixer(uc, pool_w, pool_scale), gdn_output(o_c, zc, gdn_norm)], axis=-1)
    return y_ctx, y_lat


def swiglu(h, wg, wu, wd):
    return (jax.nn.silu(h @ wg) * (h @ wu)) @ wd


def moe_swiglu(h, router_w, wg, wu, wd):
    logits = jnp.einsum('btd,de->bte', h, router_w).astype(jnp.float32)
    top_val, top_idx = lax.top_k(logits, TOP_K)
    top_w = jax.nn.softmax(top_val, axis=-1)
    gate = jnp.sum(jax.nn.one_hot(top_idx, N_EXPERTS, dtype=jnp.float32) * top_w[..., None], axis=-2)
    out = jnp.zeros(h.shape, jnp.float32)
    for e in range(N_EXPERTS):
        out = out + gate[..., e:e + 1] * swiglu(h, wg[e], wu[e], wd[e]).astype(jnp.float32)
    return out.astype(h.dtype)


def setup_inputs(seed: int = 0) -> dict:
    key = jax.random.key(seed)
    ks = jax.random.split(key, 26)
    f32 = jnp.float32
    nrm = lambda k, shape, s: s * jax.random.normal(k, shape, f32)
    n_dense = (DEPTH + 1) // 2
    n_moe = DEPTH // 2
    dt = jnp.exp(jax.random.uniform(ks[14], (DEPTH, 2, GDN_HEADS), f32, math.log(1e-3), math.log(1e-1)))
    return {
        "x": nrm(ks[0], (BATCH, SEQ, D_MODEL), 1.0),
        "c": nrm(ks[1], (BATCH, D_MODEL), 1.0),
        "ctx": nrm(ks[2], (BATCH, CTX_LEN, D_MODEL), 1.0),
        "c_ctx": nrm(ks[3], (D_MODEL,), 1.0),
        "w_mod": nrm(ks[4], (DEPTH, D_MODEL, 6 * D_MODEL), 0.5 * D_MODEL ** -0.5),
        "b_mod": nrm(ks[5], (DEPTH, 6 * D_MODEL), 0.01),
        "w_in": nrm(ks[6], (DEPTH, D_MODEL, IN_WIDTH), D_MODEL ** -0.5),
        "w_out": nrm(ks[7], (DEPTH, MIX_WIDTH, D_MODEL), MIX_WIDTH ** -0.5),
        "na_q_norm": 1 + nrm(ks[8], (DEPTH, HEAD_DIM), 0.02),
        "na_k_norm": 1 + nrm(ks[9], (DEPTH, HEAD_DIM), 0.02),
        "na_rpb": nrm(ks[10], (DEPTH, NA_HEADS, 2 * NA_WIN_R - 1, 2 * NA_WIN_C - 1), 0.05),
        "pool_w": nrm(ks[11], (DEPTH, len(POOL_WINDOWS), POOL_GROUP, POOL_GROUP), POOL_GROUP ** -0.5),
        "pool_scale": 1 + nrm(ks[12], (DEPTH, POOL_WIDTH), 0.02),
        "gdn_conv": nrm(ks[13], (DEPTH, CONV_K, 3 * GDN_WIDTH), CONV_K ** -0.5),
        "gdn_a_log": jnp.log(jax.random.uniform(ks[15], (DEPTH, 2, GDN_HEADS), f32, 1.0, 16.0)),
        "gdn_dt_bias": dt + jnp.log(-jnp.expm1(-dt)),
        "gdn_norm": 1 + nrm(ks[16], (DEPTH, HEAD_DIM), 0.02),
        "ffn_w_gate": nrm(ks[17], (n_dense, D_MODEL, D_FF), D_MODEL ** -0.5),
        "ffn_w_up": nrm(ks[18], (n_dense, D_MODEL, D_FF), D_MODEL ** -0.5),
        "ffn_w_down": nrm(ks[19], (n_dense, D_FF, D_MODEL), D_FF ** -0.5),
        "moe_router": nrm(ks[20], (n_moe, D_MODEL, N_EXPERTS), D_MODEL ** -0.5),
        "moe_w_gate": nrm(ks[21], (n_moe, N_EXPERTS, D_MODEL, D_FF_EXPERT), D_MODEL ** -0.5),
        "moe_w_up": nrm(ks[22], (n_moe, N_EXPERTS, D_MODEL, D_FF_EXPERT), D_MODEL ** -0.5),
        "moe_w_down": nrm(ks[23], (n_moe, N_EXPERTS, D_FF_EXPERT, D_MODEL), D_FF_EXPERT ** -0.5),
    }


def reference(x, c, ctx, c_ctx, w_mod, b_mod, w_in, w_out, na_q_norm, na_k_norm, na_rpb, pool_w, pool_scale,
              gdn_conv, gdn_a_log, gdn_dt_bias, gdn_norm, ffn_w_gate, ffn_w_up, ffn_w_down,
              moe_router, moe_w_gate, moe_w_up, moe_w_down):
    n_ctx = ctx.shape[1]
    seq = x.shape[1]
    c_act = jax.nn.silu(c)
    cc_act = jax.nn.silu(c_ctx)
    cx = ctx
    for i in range(DEPTH):
        last = i == DEPTH - 1
        m_x = jnp.split((c_act @ w_mod[i] + b_mod[i])[:, None, :], 6, axis=-1)
        m_c = jnp.split(cc_act @ w_mod[i] + b_mod[i], 6, axis=-1)
        h = jnp.concatenate([modulate(cx, m_c[0], m_c[1]), modulate(x, m_x[0], m_x[1])], axis=1)
        proj = h @ w_in[i]
        y_c, y_x = token_mixers(proj[:, :n_ctx], proj[:, n_ctx:], na_q_norm[i], na_k_norm[i], na_rpb[i],
                                pool_w[i], pool_scale[i], gdn_conv[i], gdn_a_log[i], gdn_dt_bias[i],
                                gdn_norm[i], not last)
        x = x + m_x[2] * (y_x @ w_out[i])
        h_x = modulate(x, m_x[3], m_x[4])
        if last:
            h = h_x
        else:
            cx = cx + m_c[2] * (y_c @ w_out[i])
            h = jnp.concatenate([modulate(cx, m_c[3], m_c[4]), h_x], axis=1)
        if i % 2 == 0:
            f = swiglu(h, ffn_w_gate[i // 2], ffn_w_up[i // 2], ffn_w_down[i // 2])
        else:
            f = moe_swiglu(h, moe_router[i // 2], moe_w_gate[i // 2], moe_w_up[i // 2], moe_w_down[i // 2])
        x = x + m_x[5] * f[:, -seq:]
        if not last:
            cx = cx + m_c[5] * f[:, :n_ctx]
    return x
```

```python
import functools

import numpy as np
import jax
import jax.numpy as jnp
from jax import lax
from jax.experimental import pallas as pl
from jax.experimental.pallas import tpu as pltpu

F32 = jnp.float32
BF16 = jnp.bfloat16

EPS = 1e-6
HEAD_DIM = 128
GRID_W = 64
NA_HEADS = 6
NA_WIN_R = 8
NA_WIN_C = 16
NA_Q_ROWS = 4
NA_K_ROWS = NA_Q_ROWS + NA_WIN_R
POOL_WINDOWS = (2, 4, 8, 16)
GDN_HEADS = 6
GDN_CHUNK = 64
N_EXPERTS = 8
LANES = 128
MASK_VALUE = -1e30
VMEM_LIMIT = 56 * 1024 * 1024
SEQ_TILE = 256
TOK_TILE = 512


def _cparams(*sem):
    return pltpu.CompilerParams(dimension_semantics=sem, vmem_limit_bytes=VMEM_LIMIT)


def _silu(x):
    return x / (1.0 + jnp.exp(-x))


def _dot(a, b):
    return jnp.dot(a, b, preferred_element_type=F32)


def _dot_nt(a, b):
    return lax.dot_general(a, b, (((1,), (1,)), ((), ())), preferred_element_type=F32)


def _dot_tn(a, b):
    return lax.dot_general(a, b, (((0,), (0,)), ((), ())), preferred_element_type=F32)


def _rms(x, w):
    xf = x.astype(F32)
    return xf * lax.rsqrt(jnp.mean(xf * xf, axis=-1, keepdims=True) + EPS) * w


def _mod_kernel(a_ref, w_ref, b_ref, o_ref):
    a = _silu(a_ref[...])
    o_ref[0] = _dot(a.astype(BF16), w_ref[0].astype(BF16)) + b_ref[0]


def _modulation(c_rows, w_mod, b_mod):
    depth, d, n = w_mod.shape
    r = c_rows.shape[0]
    tn = 1024
    return pl.pallas_call(
        _mod_kernel,
        grid=(depth, n // tn),
        in_specs=[pl.BlockSpec((r, d), lambda i, j: (0, 0)),
                  pl.BlockSpec((1, d, tn), lambda i, j: (i, 0, j)),
                  pl.BlockSpec((1, 1, tn), lambda i, j: (i, 0, j))],
        out_specs=pl.BlockSpec((1, r, tn), lambda i, j: (i, 0, j)),
        out_shape=jax.ShapeDtypeStruct((depth, r, n), F32),
        compiler_params=_cparams("parallel", "parallel"),
        name="modulation",
    )(c_rows, w_mod, b_mod.reshape(depth, 1, n))


def _norm_mod_kernel(si, x_ref, m_ref, o_ref):
    x = x_ref[0]
    y = x * lax.rsqrt(jnp.mean(x * x, axis=-1, keepdims=True) + EPS)
    o_ref[0] = (y * (1.0 + m_ref[0, si + 1:si + 2, :]) + m_ref[0, si:si + 1, :]).astype(BF16)


def _norm_mod(x, mod, si):
    g, l, d = x.shape
    tl = min(l, TOK_TILE)
    return pl.pallas_call(
        functools.partial(_norm_mod_kernel, si),
        grid=(g, l // tl),
        in_specs=[pl.BlockSpec((1, tl, d), lambda b, t: (b, t, 0)),
                  pl.BlockSpec((1, 6, d), lambda b, t: (b, 0, 0))],
        out_specs=pl.BlockSpec((1, tl, d), lambda b, t: (b, t, 0)),
        out_shape=jax.ShapeDtypeStruct((g, l, d), BF16),
        compiler_params=_cparams("parallel", "parallel"),
        name="norm_mod",
    )(x, mod)


def _mm_kernel(a_ref, w_ref, o_ref):
    o_ref[...] = _dot(a_ref[...], w_ref[...]).astype(o_ref.dtype)


def _mm(a, w, out_dtype, tm, tn):
    m, k = a.shape
    n = w.shape[1]
    tm = min(tm, m)
    return pl.pallas_call(
        _mm_kernel,
        grid=(n // tn, m // tm),
        in_specs=[pl.BlockSpec((tm, k), lambda j, i: (i, 0)),
                  pl.BlockSpec((k, tn), lambda j, i: (0, j))],
        out_specs=pl.BlockSpec((tm, tn), lambda j, i: (i, j)),
        out_shape=jax.ShapeDtypeStruct((m, n), out_dtype),
        compiler_params=_cparams("parallel", "parallel"),
        name="matmul",
    )(a, w)


def _out_proj_kernel(gi, x_ref, m_ref, y0_ref, y1_ref, y2_ref, w0_ref, w1_ref, w2_ref, o_ref):
    acc = _dot(y0_ref[0], w0_ref[...]) + _dot(y1_ref[0], w1_ref[...]) + _dot(y2_ref[0], w2_ref[...])
    o_ref[0] = x_ref[0] + m_ref[0, gi:gi + 1, :] * acc


def _out_proj(x, mod, gi, ys, ws):
    g, l, d = x.shape
    tl = min(l, TOK_TILE)
    y_specs = [pl.BlockSpec((1, tl, y.shape[2]), lambda b, t: (b, t, 0)) for y in ys]
    w_specs = [pl.BlockSpec(w.shape, lambda b, t: (0, 0)) for w in ws]
    return pl.pallas_call(
        functools.partial(_out_proj_kernel, gi),
        grid=(g, l // tl),
        in_specs=[pl.BlockSpec((1, tl, d), lambda b, t: (b, t, 0)),
                  pl.BlockSpec((1, 6, d), lambda b, t: (b, 0, 0))] + y_specs + w_specs,
        out_specs=pl.BlockSpec((1, tl, d), lambda b, t: (b, t, 0)),
        out_shape=jax.ShapeDtypeStruct((g, l, d), F32),
        compiler_params=_cparams("parallel", "parallel"),
        name="out_proj",
    )(x, mod, *ys, *ws)


def _ffn_kernel(gi, x_ref, m_ref, h_ref, wg_ref, wu_ref, wd_ref, o_ref, acc_ref):
    f = pl.program_id(2)
    h = h_ref[0]
    a = _silu(_dot(h, wg_ref[...])) * _dot(h, wu_ref[...])
    part = _dot(a.astype(BF16), wd_ref[...])

    @pl.when(f == 0)
    def _():
        acc_ref[...] = part

    @pl.when(f > 0)
    def _():
        acc_ref[...] += part

    @pl.when(f == pl.num_programs(2) - 1)
    def _():
        o_ref[0] = x_ref[0] + m_ref[0, gi:gi + 1, :] * acc_ref[...]


def _ffn(x, mod, gi, h, wg, wu, wd):
    g, l, d = x.shape
    dff = wg.shape[1]
    tl = min(l, TOK_TILE)
    tf = 512
    return pl.pallas_call(
        functools.partial(_ffn_kernel, gi),
        grid=(g, l // tl, dff // tf),
        in_specs=[pl.BlockSpec((1, tl, d), lambda b, t, f: (b, t, 0)),
                  pl.BlockSpec((1, 6, d), lambda b, t, f: (b, 0, 0)),
                  pl.BlockSpec((1, tl, d), lambda b, t, f: (b, t, 0)),
                  pl.BlockSpec((d, tf), lambda b, t, f: (0, f)),
                  pl.BlockSpec((d, tf), lambda b, t, f: (0, f)),
                  pl.BlockSpec((tf, d), lambda b, t, f: (f, 0))],
        out_specs=pl.BlockSpec((1, tl, d), lambda b, t, f: (b, t, 0)),
        out_shape=jax.ShapeDtypeStruct((g, l, d), F32),
        scratch_shapes=[pltpu.VMEM((tl, d), F32)],
        compiler_params=_cparams("parallel", "parallel", "arbitrary"),
        name="ffn",
    )(x, mod, h, wg, wu, wd)


def _router_kernel(h_ref, w_ref, o_ref):
    logits = _dot(h_ref[0], w_ref[...])
    lane = lax.broadcasted_iota(jnp.int32, logits.shape, 1).astype(F32)
    l1 = jnp.where(lane < N_EXPERTS, logits, -jnp.inf)
    m1 = jnp.max(l1, axis=-1, keepdims=True)
    i1 = jnp.min(jnp.where(l1 == m1, lane, float(LANES)), axis=-1, keepdims=True)
    l2 = jnp.where(lane == i1, -jnp.inf, l1)
    m2 = jnp.max(l2, axis=-1, keepdims=True)
    i2 = jnp.min(jnp.where(l2 == m2, lane, float(LANES)), axis=-1, keepdims=True)
    e2 = jnp.exp(m2 - m1)
    o_ref[0] = jnp.where(lane == i1, 1.0 / (1.0 + e2), jnp.where(lane == i2, e2 / (1.0 + e2), 0.0))


def _router(h, w_pad):
    g, l, d = h.shape
    tl = min(l, TOK_TILE)
    return pl.pallas_call(
        _router_kernel,
        grid=(g, l // tl),
        in_specs=[pl.BlockSpec((1, tl, d), lambda b, t: (b, t, 0)),
                  pl.BlockSpec((d, LANES), lambda b, t: (0, 0))],
        out_specs=pl.BlockSpec((1, tl, LANES), lambda b, t: (b, t, 0)),
        out_shape=jax.ShapeDtypeStruct((g, l, LANES), F32),
        compiler_params=_cparams("parallel", "parallel"),
        name="router",
    )(h, w_pad)


def _moe_kernel(gi, x_ref, m_ref, h_ref, gate_ref, wg_ref, wu_ref, wd_ref, o_ref, acc_ref):
    e = pl.program_id(2)
    f = pl.program_id(3)
    h = h_ref[0]
    gate = gate_ref[0]
    lane = lax.broadcasted_iota(jnp.int32, gate.shape, 1)
    ge = jnp.sum(jnp.where(lane == e, gate, 0.0), axis=-1, keepdims=True)
    a = _silu(_dot(h, wg_ref[0])) * _dot(h, wu_ref[0]) * ge
    part = _dot(a.astype(BF16), wd_ref[0])
    first = jnp.logical_and(e == 0, f == 0)
    last = jnp.logical_and(e == pl.num_programs(2) - 1, f == pl.num_programs(3) - 1)

    @pl.when(first)
    def _():
        acc_ref[...] = part

    @pl.when(jnp.logical_not(first))
    def _():
        acc_ref[...] += part

    @pl.when(last)
    def _():
        o_ref[0] = x_ref[0] + m_ref[0, gi:gi + 1, :] * acc_ref[...]


def _moe(x, mod, gi, h, gate, wg, wu, wd):
    g, l, d = x.shape
    ne, _, dff = wg.shape
    tl = min(l, TOK_TILE)
    tf = 512
    return pl.pallas_call(
        functools.partial(_moe_kernel, gi),
        grid=(g, l // tl, ne, dff // tf),
        in_specs=[pl.BlockSpec((1, tl, d), lambda b, t, e, f: (b, t, 0)),
                  pl.BlockSpec((1, 6, d), lambda b, t, e, f: (b, 0, 0)),
                  pl.BlockSpec((1, tl, d), lambda b, t, e, f: (b, t, 0)),
                  pl.BlockSpec((1, tl, LANES), lambda b, t, e, f: (b, t, 0)),
                  pl.BlockSpec((1, d, tf), lambda b, t, e, f: (e, 0, f)),
                  pl.BlockSpec((1, d, tf), lambda b, t, e, f: (e, 0, f)),
                  pl.BlockSpec((1, tf, d), lambda b, t, e, f: (e, f, 0))],
        out_specs=pl.BlockSpec((1, tl, d), lambda b, t, e, f: (b, t, 0)),
        out_shape=jax.ShapeDtypeStruct((g, l, d), F32),
        scratch_shapes=[pltpu.VMEM((tl, d), F32)],
        compiler_params=_cparams("parallel", "parallel", "arbitrary", "arbitrary"),
        name="moe",
    )(x, mod, h, gate, wg, wu, wd)


def _na_bias_table(rpb, rows):
    w = GRID_W
    col = np.arange(w)
    c0 = np.clip(col - NA_WIN_C // 2, 0, w - NA_WIN_C)
    col_in = (col[None, :] >= c0[:, None]) & (col[None, :] < c0[:, None] + NA_WIN_C)
    col_idx = np.clip(col[None, :] - col[:, None], 1 - NA_WIN_C, NA_WIN_C - 1) + NA_WIN_C - 1
    n_groups = rows // NA_Q_ROWS
    dr = np.zeros((3, NA_Q_ROWS * w, NA_K_ROWS * w), np.int32)
    ci = np.zeros((3, NA_Q_ROWS * w, NA_K_ROWS * w), np.int32)
    ok = np.zeros((3, NA_Q_ROWS * w, NA_K_ROWS * w), bool)
    for variant, grp in enumerate((0, 1, n_groups - 1)):
        r = grp * NA_Q_ROWS
        start = _na_key_start(r, rows)
        for a in range(NA_Q_ROWS):
            ws = int(np.clip(r + a - NA_WIN_R // 2, 0, rows - NA_WIN_R))
            for i in range(NA_K_ROWS):
                kr = start + i
                qs = slice(a * w, (a + 1) * w)
                ks = slice(i * w, (i + 1) * w)
                row_ok = ws <= kr < ws + NA_WIN_R
                dr[variant, qs, ks] = np.clip(kr - (r + a) + NA_WIN_R - 1, 0, 2 * NA_WIN_R - 2)
                ci[variant, qs, ks] = col_idx
                ok[variant, qs, ks] = col_in & row_ok
    bias = rpb[:, dr, ci]
    return jnp.where(ok[None], bias, MASK_VALUE).astype(F32)


def _na_key_start(r, rows):
    return int(np.clip(r - NA_WIN_R // 2, 0, rows - NA_K_ROWS))


def _na_kernel(ctx_out, q_ref, k_ref, v_ref, qc_ref, kc_ref, vc_ref, bias_ref, qw_ref, kw_ref, *rest):
    if ctx_out:
        o_ref, oc_ref, kn_ref = rest
    else:
        o_ref, kn_ref = rest
    n_ctx = kc_ref.shape[1]
    seq = k_ref.shape[1]
    rows = seq // GRID_W
    n_groups = rows // NA_Q_ROWS
    nq = NA_Q_ROWS * GRID_W
    nk = NA_K_ROWS * GRID_W
    scale = HEAD_DIM ** -0.5
    qw = qw_ref[...]
    kw = kw_ref[...]
    kn_ref[0:n_ctx, :] = _rms(kc_ref[0], kw).astype(BF16)

    def norm_keys(j, carry):
        r0 = pl.multiple_of(j * nq, nq)
        kn_ref[pl.ds(n_ctx + r0, nq), :] = _rms(k_ref[0, pl.ds(r0, nq), :], kw).astype(BF16)
        return carry

    lax.fori_loop(0, seq // nq, norm_keys, 0)
    k_ctx = kn_ref[0:n_ctx, :]
    v_ctx = vc_ref[0]

    def group(g, carry):
        q0 = pl.multiple_of(g * nq, nq)
        k0 = pl.multiple_of(jnp.clip(g * NA_Q_ROWS - NA_WIN_R // 2, 0, rows - NA_K_ROWS) * GRID_W, GRID_W)
        variant = jnp.where(g == 0, 0, jnp.where(g == n_groups - 1, 2, 1))
        qn = (_rms(q_ref[0, pl.ds(q0, nq), :], qw) * scale).astype(BF16)
        s = _dot_nt(qn, kn_ref[pl.ds(n_ctx + k0, nk), :]) + bias_ref[0, variant]
        sc = _dot_nt(qn, k_ctx)
        m = jnp.maximum(jnp.max(s, axis=-1, keepdims=True), jnp.max(sc, axis=-1, keepdims=True))
        p = jnp.exp(s - m)
        pc = jnp.exp(sc - m)
        denom = jnp.sum(p, axis=-1, keepdims=True) + jnp.sum(pc, axis=-1, keepdims=True)
        o = _dot(p.astype(BF16), v_ref[0, pl.ds(k0, nk), :]) + _dot(pc.astype(BF16), v_ctx)
        o_ref[0, pl.ds(q0, nq), :] = (o / denom).astype(o_ref.dtype)
        return carry

    lax.fori_loop(0, n_groups, group, 0)

    if ctx_out:
        qn = (_rms(qc_ref[0], qw) * scale).astype(BF16)
        sc = _dot_nt(qn, k_ctx)
        pc = jnp.exp(sc - jnp.max(sc, axis=-1, keepdims=True))
        o = _dot(pc.astype(BF16), v_ctx) / jnp.sum(pc, axis=-1, keepdims=True)
        oc_ref[0] = o.astype(oc_ref.dtype)


def _na(proj_l, proj_c, bias, q_norm, k_norm, ctx_out):
    bsz, seq, _ = proj_l.shape
    n_ctx = proj_c.shape[1]
    h_ = NA_HEADS
    hd = HEAD_DIM
    lat = lambda off: pl.BlockSpec((1, seq, hd), lambda h, b: (b, 0, off + h))
    ctx = lambda off: pl.BlockSpec((1, n_ctx, hd), lambda h, b: (b, 0, off + h))
    out_shape = [jax.ShapeDtypeStruct((bsz, seq, h_ * hd), BF16)]
    out_specs = [pl.BlockSpec((1, seq, hd), lambda h, b: (b, 0, h))]
    if ctx_out:
        out_shape.append(jax.ShapeDtypeStruct((bsz, n_ctx, h_ * hd), BF16))
        out_specs.append(pl.BlockSpec((1, n_ctx, hd), lambda h, b: (b, 0, h)))
    res = pl.pallas_call(
        functools.partial(_na_kernel, ctx_out),
        grid=(h_, bsz),
        in_specs=[lat(0), lat(h_), lat(2 * h_), ctx(0), ctx(h_), ctx(2 * h_),
                  pl.BlockSpec((1,) + bias.shape[1:], lambda h, b: (h, 0, 0, 0)),
                  pl.BlockSpec((1, hd), lambda h, b: (0, 0)),
                  pl.BlockSpec((1, hd), lambda h, b: (0, 0))],
        out_specs=out_specs,
        out_shape=out_shape,
        scratch_shapes=[pltpu.VMEM((n_ctx + seq, hd), BF16)],
        compiler_params=_cparams("parallel", "parallel"),
        name="neighbourhood_attention",
    )(proj_l, proj_l, proj_l, proj_c, proj_c, proj_c, bias, q_norm.reshape(1, hd), k_norm.reshape(1, hd))
    return (res[0], res[1]) if ctx_out else (res[0], None)


def _pool_bands():
    t = SEQ_TILE
    i = np.arange(t)[:, None]
    j = np.arange(t)[None, :]
    bands = np.zeros((len(POOL_WINDOWS), 3, t, t), np.float32)
    for gi, win in enumerate(POOL_WINDOWS):
        half = win // 2
        for s, off in enumerate((-t, 0, t)):
            jj = j + off
            bands[gi, s] = (jj >= i - half) & (jj <= i + half - 1)
    return jnp.asarray(bands, BF16)


def _pool_kernel(seq_len, u_ref, up_ref, un_ref, band_ref, pw_ref, sc_ref, o_ref):
    t = pl.program_id(1)
    nt = pl.num_programs(1)
    tl = u_ref.shape[1]
    half = lax.shift_left(1, pl.program_id(2))
    has_prev = (t > 0).astype(F32)
    has_next = (t < nt - 1).astype(F32)
    pos = t * tl + lax.broadcasted_iota(jnp.int32, (tl, 1), 0)
    cur = u_ref[0]
    wsum = (_dot(band_ref[0, 1], cur) + has_prev * _dot(band_ref[0, 0], up_ref[0])
            + has_next * _dot(band_ref[0, 2], un_ref[0]))
    cnt = jnp.minimum(pos + half, seq_len) - jnp.maximum(pos - half, 0)
    d = wsum / cnt.astype(F32) - cur.astype(F32)
    o_ref[0] = (_dot(d.astype(BF16), pw_ref[0]) * sc_ref[...]).astype(o_ref.dtype)


def _pool(proj, bands, pool_w, pool_scale):
    assert POOL_WINDOWS == tuple(2 << g for g in range(len(POOL_WINDOWS)))
    bsz, l, _ = proj.shape
    tl = SEQ_TILE
    nt = l // tl
    ng = len(POOL_WINDOWS)
    cb = POOL_COL // LANES
    return pl.pallas_call(
        functools.partial(_pool_kernel, l),
        grid=(bsz, nt, ng),
        in_specs=[pl.BlockSpec((1, tl, LANES), lambda b, t, g: (b, t, cb + g)),
                  pl.BlockSpec((1, tl, LANES), lambda b, t, g: (b, jnp.maximum(t - 1, 0), cb + g)),
                  pl.BlockSpec((1, tl, LANES), lambda b, t, g: (b, jnp.minimum(t + 1, nt - 1), cb + g)),
                  pl.BlockSpec((1,) + bands.shape[1:], lambda b, t, g: (g, 0, 0, 0)),
                  pl.BlockSpec((1, LANES, LANES), lambda b, t, g: (g, 0, 0)),
                  pl.BlockSpec((1, LANES), lambda b, t, g: (0, g))],
        out_specs=pl.BlockSpec((1, tl, LANES), lambda b, t, g: (b, t, g)),
        out_shape=jax.ShapeDtypeStruct((bsz, l, ng * LANES), BF16),
        compiler_params=_cparams("parallel", "parallel", "parallel"),
        name="pool_mixer",
    )(proj, proj, proj, bands, pool_w, pool_scale.reshape(1, ng * LANES))


def _gdn_prep_kernel(x_ref, xp_ref, xn_ref, w_ref, o_ref):
    t = pl.program_id(1)
    j = pl.program_id(2)
    tl = x_ref.shape[1]
    halo = xp_ref.shape[1]
    x = x_ref[0].astype(F32)
    row = lax.broadcasted_iota(jnp.int32, x.shape, 0)
    prev_row = jnp.where(t > 0, xp_ref[0, halo - 1:halo, :].astype(F32), 0.0)
    next_row = jnp.where(t < pl.num_programs(1) - 1, xn_ref[0, 0:1, :].astype(F32), 0.0)
    x_m1 = jnp.where(row == 0, prev_row, pltpu.roll(x, 1, axis=0))
    x_p1 = jnp.where(row == tl - 1, next_row, pltpu.roll(x, tl - 1, axis=0))
    y = _silu(w_ref[0:1, :] * x_m1 + w_ref[1:2, :] * x + w_ref[2:3, :] * x_p1)
    third = pl.num_programs(2) // 3
    for hh in range(y.shape[1] // HEAD_DIM):
        cs = slice(hh * HEAD_DIM, (hh + 1) * HEAD_DIM)
        yh = y[:, cs]
        inv = lax.rsqrt(jnp.sum(yh * yh, axis=-1, keepdims=True) + EPS)
        fac = jnp.where(j < third, inv * HEAD_DIM ** -0.5, jnp.where(j < 2 * third, inv, 1.0))
        o_ref[0, :, cs] = (yh * fac).astype(o_ref.dtype)


def _gdn_prep(proj, conv_w):
    bsz, l, _ = proj.shape
    width = conv_w.shape[1]
    tl = SEQ_TILE
    tc = 2 * HEAD_DIM
    halo = 16
    nt = l // tl
    cb = GDN_COL // tc
    nh = l // halo
    return pl.pallas_call(
        _gdn_prep_kernel,
        grid=(bsz, nt, width // tc),
        in_specs=[pl.BlockSpec((1, tl, tc), lambda b, t, j: (b, t, cb + j)),
                  pl.BlockSpec((1, halo, tc), lambda b, t, j: (b, jnp.maximum(t * (tl // halo) - 1, 0), cb + j)),
                  pl.BlockSpec((1, halo, tc), lambda b, t, j: (b, jnp.minimum((t + 1) * (tl // halo), nh - 1), cb + j)),
                  pl.BlockSpec((3, tc), lambda b, t, j: (0, j))],
        out_specs=pl.BlockSpec((1, tl, tc), lambda b, t, j: (b, t, j)),
        out_shape=jax.ShapeDtypeStruct((bsz, l, width), BF16),
        compiler_params=_cparams("parallel", "parallel", "parallel"),
        name="gdn_prep",
    )(proj, proj, proj, conv_w)


def _gdn_gates_kernel(ab_ref, neg_a_ref, dt_ref, is_a_ref, is_bwd_ref, o_ref):
    x = ab_ref[0]
    tl = x.shape[0]
    z = x + dt_ref[...]
    g = neg_a_ref[...] * (jnp.maximum(z, 0.0) + jnp.log(1.0 + jnp.exp(-jnp.abs(z))))
    beta = 1.0 / (1.0 + jnp.exp(-x))
    pos = lax.broadcasted_iota(jnp.int32, x.shape, 0) % GDN_CHUNK
    fwd = g
    bwd = g
    s = 1
    while s < GDN_CHUNK:
        fwd = fwd + jnp.where(pos >= s, pltpu.roll(fwd, s, axis=0), 0.0)
        bwd = bwd + jnp.where(pos < GDN_CHUNK - s, pltpu.roll(bwd, tl - s, axis=0), 0.0)
        s *= 2
    gc = jnp.where(is_bwd_ref[...] > 0, bwd, fwd)
    o_ref[0] = jnp.where(is_a_ref[...] > 0, gc, beta)


def _gdn_gates(ab, a_log, dt_bias):
    bsz, l, _ = ab.shape
    nh = GDN_HEADS
    neg_a = jnp.zeros((2, 2, nh), F32).at[:, 0].set(-jnp.exp(a_log.astype(F32)))
    dt = jnp.zeros((2, 2, nh), F32).at[:, 0].set(dt_bias.astype(F32))
    is_a = jnp.zeros((2, 2, nh), F32).at[:, 0].set(1.0)
    is_bwd = jnp.zeros((2, 2, nh), F32).at[1].set(1.0)
    lanes = lambda v: jnp.pad(v.reshape(1, 4 * nh), ((0, 0), (0, LANES - 4 * nh)))
    tl = SEQ_TILE
    vec = pl.BlockSpec((1, LANES), lambda b, t: (0, 0))
    return pl.pallas_call(
        _gdn_gates_kernel,
        grid=(bsz, l // tl),
        in_specs=[pl.BlockSpec((1, tl, LANES), lambda b, t: (b, t, 0)), vec, vec, vec, vec],
        out_specs=pl.BlockSpec((1, tl, LANES), lambda b, t: (b, t, 0)),
        out_shape=jax.ShapeDtypeStruct((bsz, l, LANES), F32),
        compiler_params=_cparams("parallel", "parallel"),
        name="gdn_gates",
    )(ab, lanes(neg_a), lanes(dt), lanes(is_a), lanes(is_bwd))


def _gdn_local_kernel(q_ref, k_ref, v_ref, gcol_ref, grow_ref, uw_f_ref, qk_f_ref, at_f_ref,
                      uw_b_ref, qk_b_ref, at_b_ref):
    cs = GDN_CHUNK
    ii = lax.broadcasted_iota(jnp.int32, (cs, cs), 0)
    jj = lax.broadcasted_iota(jnp.int32, (cs, cs), 1)
    eye = (ii == jj).astype(F32)
    outs = ((uw_f_ref, qk_f_ref, at_f_ref), (uw_b_ref, qk_b_ref, at_b_ref))
    for c in range(q_ref.shape[1] // cs):
        rs = slice(c * cs, (c + 1) * cs)
        q = q_ref[0, rs, :].astype(F32)
        k = k_ref[0, rs, :].astype(F32)
        v = v_ref[0, rs, :].astype(F32)
        k16 = k.astype(BF16)
        qk = _dot_nt(q.astype(BF16), k16)
        for d in range(2):
            uw_ref, qkd_ref, at_ref = outs[d]
            gc = gcol_ref[0, 0, rs, 2 * d:2 * d + 1]
            beta = gcol_ref[0, 0, rs, 2 * d + 1:2 * d + 2]
            gr = grow_ref[0, 0, 2 * d:2 * d + 1, rs]
            incl = (ii >= jj) if d == 0 else (ii <= jj)
            strict = (ii > jj) if d == 0 else (ii < jj)
            decay = jnp.where(incl, jnp.exp(jnp.where(incl, gc - gr, 0.0)), 0.0)
            kb = k * beta
            a_mat = jnp.where(strict, _dot_nt(kb.astype(BF16), k16) * decay, 0.0)
            pw = -a_mat
            inv = eye + pw
            s = 2
            while s < cs:
                p16 = pw.astype(BF16)
                pw = _dot(p16, p16)
                inv = inv + _dot(inv.astype(BF16), pw.astype(BF16))
                s *= 2
            e_gc = jnp.exp(gc)
            rhs = jnp.concatenate([v * beta, kb * e_gc], axis=-1)
            uw_ref[0, 0, rs, :] = _dot(inv.astype(BF16), rhs.astype(BF16)).astype(uw_ref.dtype)
            g_last = gc[cs - 1:cs, :] if d == 0 else gc[0:1, :]
            qkd_ref[0, 0, rs, :] = jnp.concatenate([q * e_gc, k * jnp.exp(g_last - gc)], axis=-1).astype(qkd_ref.dtype)
            at_ref[0, 0, rs, :] = (qk * decay).astype(at_ref.dtype)


def _gdn_local(qkv, gcol, grow):
    bsz, l, _ = qkv.shape
    nh = GDN_HEADS
    hd = HEAD_DIM
    tl = SEQ_TILE
    col = lambda off: pl.BlockSpec((1, tl, hd), lambda b, h, t: (b, t, off + h))
    wide = pl.BlockSpec((1, 1, tl, 2 * hd), lambda b, h, t: (b, h, t, 0))
    att = pl.BlockSpec((1, 1, tl, GDN_CHUNK), lambda b, h, t: (b, h, t, 0))
    s_wide = jax.ShapeDtypeStruct((bsz, nh, l, 2 * hd), BF16)
    s_att = jax.ShapeDtypeStruct((bsz, nh, l, GDN_CHUNK), BF16)
    return pl.pallas_call(
        _gdn_local_kernel,
        grid=(bsz, nh, l // tl),
        in_specs=[col(0), col(nh), col(2 * nh),
                  pl.BlockSpec((1, 1, tl, 4), lambda b, h, t: (b, h, t, 0)),
                  pl.BlockSpec((1, 1, 4, tl), lambda b, h, t: (b, h, 0, t))],
        out_specs=[wide, wide, att, wide, wide, att],
        out_shape=[s_wide, s_wide, s_att, s_wide, s_wide, s_att],
        compiler_params=_cparams("parallel", "parallel", "parallel"),
        name="gdn_local",
    )(qkv, qkv, qkv, gcol, grow)


def _gdn_scan_kernel(ctx_out, *refs):
    lat = refs[0:8]
    ctx = refs[8:16]
    nw_ref = refs[16]
    if ctx_out:
        o_ref, oc_ref, of_ref, ob_ref, ocf_ref, ocb_ref = refs[17:]
    else:
        o_ref, of_ref, ob_ref = refs[17:]
        oc_ref = ocf_ref = ocb_ref = None
    cs = GDN_CHUNK
    hd = HEAD_DIM

    def scan(seq_refs, out_f, out_b, state):
        uw_f, qk_f, at_f, uw_b, qk_b, at_b, gcol, _ = seq_refs
        n = uw_f.shape[2] // cs

        def one(uw, qk, at, r0, g_row, g_lane, s, out):
            s16 = s.astype(BF16)
            v_new = uw[0, 0, pl.ds(r0, cs), 0:hd].astype(F32) - _dot(uw[0, 0, pl.ds(r0, cs), hd:2 * hd], s16)
            vn16 = v_new.astype(BF16)
            if out is not None:
                out[pl.ds(r0, cs), :] = (_dot(qk[0, 0, pl.ds(r0, cs), 0:hd], s16)
                                         + _dot(at[0, 0, pl.ds(r0, cs), :], vn16))
            e_last = jnp.exp(gcol[0, 0, pl.ds(g_row, 1), g_lane:g_lane + 1])
            return s * e_last + _dot_tn(qk[0, 0, pl.ds(r0, cs), hd:2 * hd], vn16)

        def step(i, st):
            s_f, s_b = st
            rf = pl.multiple_of(i * cs, cs)
            rb = pl.multiple_of((n - 1 - i) * cs, cs)
            s_f = one(uw_f, qk_f, at_f, rf, rf + cs - 1, 0, s_f, out_f)
            s_b = one(uw_b, qk_b, at_b, rb, rb, 2, s_b, out_b)
            return s_f, s_b

        return lax.fori_loop(0, n, step, state)

    def finish(seq_refs, out_f, out_b, dst):
        z = seq_refs[7][0].astype(F32)
        o = _rms(out_f[...] + out_b[...], nw_ref[...])
        dst[0] = (o * _silu(z)).astype(dst.dtype)

    zero = jnp.zeros((hd, hd), F32)
    state = scan(ctx, ocf_ref, ocb_ref, (zero, zero))
    scan(lat, of_ref, ob_ref, state)
    finish(lat, of_ref, ob_ref, o_ref)
    if ctx_out:
        finish(ctx, ocf_ref, ocb_ref, oc_ref)


def _gdn_scan(loc_l, gcol_l, proj_l, loc_c, gcol_c, proj_c, norm_w, ctx_out):
    bsz, nh, seq, _ = loc_l[0].shape
    n_ctx = loc_c[0].shape[2]
    hd = HEAD_DIM
    z_col = Z_COL // hd

    def specs(l):
        wide = pl.BlockSpec((1, 1, l, 2 * hd), lambda b, h: (b, h, 0, 0))
        att = pl.BlockSpec((1, 1, l, GDN_CHUNK), lambda b, h: (b, h, 0, 0))
        return [wide, wide, att, wide, wide, att,
                pl.BlockSpec((1, 1, l, 4), lambda b, h: (b, h, 0, 0)),
                pl.BlockSpec((1, l, hd), lambda b, h: (b, 0, z_col + h))]

    out_shape = [jax.ShapeDtypeStruct((bsz, seq, nh * hd), BF16)]
    out_specs = [pl.BlockSpec((1, seq, hd), lambda b, h: (b, 0, h))]
    scratch = [pltpu.VMEM((seq, hd), F32), pltpu.VMEM((seq, hd), F32)]
    if ctx_out:
        out_shape.append(jax.ShapeDtypeStruct((bsz, n_ctx, nh * hd), BF16))
        out_specs.append(pl.BlockSpec((1, n_ctx, hd), lambda b, h: (b, 0, h)))
        scratch += [pltpu.VMEM((n_ctx, hd), F32), pltpu.VMEM((n_ctx, hd), F32)]
    res = pl.pallas_call(
        functools.partial(_gdn_scan_kernel, ctx_out),
        grid=(bsz, nh),
        in_specs=specs(seq) + specs(n_ctx) + [pl.BlockSpec((1, hd), lambda b, h: (0, 0))],
        out_specs=out_specs,
        out_shape=out_shape,
        scratch_shapes=scratch,
        compiler_params=_cparams("parallel", "parallel"),
        name="gdn_scan",
    )(*loc_l, gcol_l, proj_l, *loc_c, gcol_c, proj_c, norm_w.reshape(1, hd))
    return (res[0], res[1]) if ctx_out else (res[0], None)


def _gate_layouts(gb):
    bsz, l, _ = gb.shape
    nh = GDN_HEADS
    t = gb[:, :, :4 * nh].reshape(bsz, l, 2, 2, nh)
    gcol = jnp.transpose(t, (0, 4, 1, 2, 3)).reshape(bsz, nh, l, 4)
    grow = jnp.transpose(t, (0, 4, 2, 3, 1)).reshape(bsz, nh, 4, l)
    return gcol, grow


NA_WIDTH = NA_HEADS * HEAD_DIM
POOL_WIDTH = LANES * len(POOL_WINDOWS)
GDN_WIDTH = GDN_HEADS * HEAD_DIM
POOL_COL = 3 * NA_WIDTH
GDN_COL = POOL_COL + POOL_WIDTH
Z_COL = GDN_COL + 3 * GDN_WIDTH
AB_COL = Z_COL + GDN_WIDTH


def _token_mixers(proj_l, ab_l, proj_c, ab_c, lw, ctx_out):
    y_na_l, y_na_c = _na(proj_l, proj_c, lw["na_bias"], lw["na_q_norm"], lw["na_k_norm"], ctx_out)
    y_pool_l = _pool(proj_l, lw["bands"], lw["pool_w"], lw["pool_scale"])
    y_pool_c = _pool(proj_c, lw["bands"], lw["pool_w"], lw["pool_scale"]) if ctx_out else None

    def local(proj, ab):
        gb = _gdn_gates(ab, lw["gdn_a_log"], lw["gdn_dt_bias"])
        gcol, grow = _gate_layouts(gb)
        return _gdn_local(_gdn_prep(proj, lw["gdn_conv"]), gcol, grow), gcol

    loc_l, gcol_l = local(proj_l, ab_l)
    loc_c, gcol_c = local(proj_c, ab_c)
    y_gdn_l, y_gdn_c = _gdn_scan(loc_l, gcol_l, proj_l, loc_c, gcol_c, proj_c, lw["gdn_norm"], ctx_out)
    return (y_na_l, y_pool_l, y_gdn_l), (y_na_c, y_pool_c, y_gdn_c)


def kernel(x, c, ctx, c_ctx, w_mod, b_mod, w_in, w_out, na_q_norm, na_k_norm, na_rpb, pool_w, pool_scale,
           gdn_conv, gdn_a_log, gdn_dt_bias, gdn_norm, ffn_w_gate, ffn_w_up, ffn_w_down,
           moe_router, moe_w_gate, moe_w_up, moe_w_down):
    bsz, seq, d = x.shape
    n_ctx = ctx.shape[1]
    depth = w_mod.shape[0]
    assert seq % (GRID_W * NA_Q_ROWS) == 0 and seq // GRID_W >= NA_K_ROWS
    assert seq % SEQ_TILE == 0 and n_ctx % SEQ_TILE == 0 and w_in.shape[2] == AB_COL + 4 * GDN_HEADS

    mod_rows = 16
    c_rows = jnp.zeros((mod_rows, d), F32).at[:bsz].set(c).at[bsz].set(c_ctx)
    mods = _modulation(c_rows, w_mod, b_mod).reshape(depth, mod_rows, 6, d)
    bands = _pool_bands()

    xl = x
    xc = ctx.reshape(1, bsz * n_ctx, d)
    for i in range(depth):
        last = i == depth - 1
        mod_l = mods[i, :bsz]
        mod_c = mods[i, bsz:bsz + 1]
        w_main = w_in[i, :, :AB_COL].astype(BF16)
        w_ab = jnp.pad(w_in[i, :, AB_COL:], ((0, 0), (0, LANES - 4 * GDN_HEADS))).astype(BF16)
        wo = w_out[i].astype(BF16)
        ws = (wo[:NA_WIDTH], wo[NA_WIDTH:NA_WIDTH + POOL_WIDTH], wo[NA_WIDTH + POOL_WIDTH:])
        lw = dict(na_bias=_na_bias_table(na_rpb[i], seq // GRID_W), na_q_norm=na_q_norm[i], na_k_norm=na_k_norm[i],
                  bands=bands, pool_w=pool_w[i].astype(BF16), pool_scale=pool_scale[i], gdn_conv=gdn_conv[i],
                  gdn_a_log=gdn_a_log[i], gdn_dt_bias=gdn_dt_bias[i], gdn_norm=gdn_norm[i])

        h_l = _norm_mod(xl, mod_l, 0).reshape(bsz * seq, d)
        h_c = _norm_mod(xc, mod_c, 0).reshape(bsz * n_ctx, d)
        half = AB_COL // 2
        proj_l = _mm(h_l, w_main, BF16, TOK_TILE, half).reshape(bsz, seq, AB_COL)
        proj_c = _mm(h_c, w_main, BF16, TOK_TILE, half).reshape(bsz, n_ctx, AB_COL)
        ab_l = _mm(h_l, w_ab, F32, 2048, LANES).reshape(bsz, seq, LANES)
        ab_c = _mm(h_c, w_ab, F32, 2048, LANES).reshape(bsz, n_ctx, LANES)
        ys_l, ys_c = _token_mixers(proj_l, ab_l, proj_c, ab_c, lw, not last)
        xl = _out_proj(xl, mod_l, 2, ys_l, ws)
        if not last:
            xc = _out_proj(xc, mod_c, 2, [y.reshape(1, bsz * n_ctx, -1) for y in ys_c], ws)

        streams = [(xl, mod_l)] if last else [(xl, mod_l), (xc, mod_c)]
        outs = []
        for xs, mod in streams:
            h = _norm_mod(xs, mod, 3)
            if i % 2 == 0:
                j = i // 2
                outs.append(_ffn(xs, mod, 5, h, ffn_w_gate[j].astype(BF16), ffn_w_up[j].astype(BF16),
                                 ffn_w_down[j].astype(BF16)))
            else:
                j = i // 2
                rw = jnp.pad(moe_router[j], ((0, 0), (0, LANES - N_EXPERTS))).astype(BF16)
                gate = _router(h, rw)
                outs.append(_moe(xs, mod, 5, h, gate, moe_w_gate[j].astype(BF16), moe_w_up[j].astype(BF16),
                                 moe_w_down[j].astype(BF16)))
        xl = outs[0]
        if not last:
            xc = outs[1]
    return xl
```

```python
import functools

import numpy as np
import jax
import jax.numpy as jnp
from jax import lax
from jax.experimental import pallas as pl
from jax.experimental.pallas import tpu as pltpu

F32 = jnp.float32
BF16 = jnp.bfloat16

EPS = 1e-6
HEAD_DIM = 128
GRID_W = 64
NA_HEADS = 6
NA_WIN_R = 8
NA_WIN_C = 16
NA_Q_ROWS = 4
NA_K_ROWS = NA_Q_ROWS + NA_WIN_R
POOL_WINDOWS = (2, 4, 8, 16)
GDN_HEADS = 6
GDN_CHUNK = 64
N_EXPERTS = 8
LANES = 128
MASK_VALUE = -1e30
VMEM_LIMIT = 56 * 1024 * 1024
SEQ_TILE = 256
TOK_TILE = 512


def _cparams(*sem):
    return pltpu.CompilerParams(dimension_semantics=sem, vmem_limit_bytes=VMEM_LIMIT)


def _silu(x):
    return x / (1.0 + jnp.exp(-x))


def _dot(a, b):
    return jnp.dot(a, b, preferred_element_type=F32)


def _dot_nt(a, b):
    return lax.dot_general(a, b, (((1,), (1,)), ((), ())), preferred_element_type=F32)


def _dot_tn(a, b):
    return lax.dot_general(a, b, (((0,), (0,)), ((), ())), preferred_element_type=F32)


def _rms(x, w):
    xf = x.astype(F32)
    return xf * lax.rsqrt(jnp.mean(xf * xf, axis=-1, keepdims=True) + EPS) * w


def _mod_kernel(a_ref, w_ref, b_ref, o_ref):
    a = _silu(a_ref[...])
    o_ref[0] = _dot(a.astype(BF16), w_ref[0].astype(BF16)) + b_ref[0]


def _modulation(c_rows, w_mod, b_mod):
    depth, d, n = w_mod.shape
    r = c_rows.shape[0]
    tn = 1024
    return pl.pallas_call(
        _mod_kernel,
        grid=(depth, n // tn),
        in_specs=[pl.BlockSpec((r, d), lambda i, j: (0, 0)),
                  pl.BlockSpec((1, d, tn), lambda i, j: (i, 0, j)),
                  pl.BlockSpec((1, 1, tn), lambda i, j: (i, 0, j))],
        out_specs=pl.BlockSpec((1, r, tn), lambda i, j: (i, 0, j)),
        out_shape=jax.ShapeDtypeStruct((depth, r, n), F32),
        compiler_params=_cparams("parallel", "parallel"),
        name="modulation",
    )(c_rows, w_mod, b_mod.reshape(depth, 1, n))


def _norm_mod_kernel(si, x_ref, m_ref, o_ref):
    x = x_ref[0]
    y = x * lax.rsqrt(jnp.mean(x * x, axis=-1, keepdims=True) + EPS)
    o_ref[0] = (y * (1.0 + m_ref[0, si + 1:si + 2, :]) + m_ref[0, si:si + 1, :]).astype(BF16)


def _norm_mod(x, mod, si):
    g, l, d = x.shape
    tl = min(l, TOK_TILE)
    return pl.pallas_call(
        functools.partial(_norm_mod_kernel, si),
        grid=(g, l // tl),
        in_specs=[pl.BlockSpec((1, tl, d), lambda b, t: (b, t, 0)),
                  pl.BlockSpec((1, 6, d), lambda b, t: (b, 0, 0))],
        out_specs=pl.BlockSpec((1, tl, d), lambda b, t: (b, t, 0)),
        out_shape=jax.ShapeDtypeStruct((g, l, d), BF16),
        compiler_params=_cparams("parallel", "parallel"),
        name="norm_mod",
    )(x, mod)


def _mm_kernel(a_ref, w_ref, o_ref):
    o_ref[...] = _dot(a_ref[...], w_ref[...]).astype(o_ref.dtype)


def _mm(a, w, out_dtype, tm, tn):
    m, k = a.shape
    n = w.shape[1]
    tm = min(tm, m)
    return pl.pallas_call(
        _mm_kernel,
        grid=(n // tn, m // tm),
        in_specs=[pl.BlockSpec((tm, k), lambda j, i: (i, 0)),
                  pl.BlockSpec((k, tn), lambda j, i: (0, j))],
        out_specs=pl.BlockSpec((tm, tn), lambda j, i: (i, j)),
        out_shape=jax.ShapeDtypeStruct((m, n), out_dtype),
        compiler_params=_cparams("parallel", "parallel"),
        name="matmul",
    )(a, w)


def _out_proj_kernel(gi, x_ref, m_ref, y0_ref, y1_ref, y2_ref, w0_ref, w1_ref, w2_ref, o_ref):
    acc = _dot(y0_ref[0], w0_ref[...]) + _dot(y1_ref[0], w1_ref[...]) + _dot(y2_ref[0], w2_ref[...])
    o_ref[0] = x_ref[0] + m_ref[0, gi:gi + 1, :] * acc


def _out_proj(x, mod, gi, ys, ws):
    g, l, d = x.shape
    tl = min(l, TOK_TILE)
    y_specs = [pl.BlockSpec((1, tl, y.shape[2]), lambda b, t: (b, t, 0)) for y in ys]
    w_specs = [pl.BlockSpec(w.shape, lambda b, t: (0, 0)) for w in ws]
    return pl.pallas_call(
        functools.partial(_out_proj_kernel, gi),
        grid=(g, l // tl),
        in_specs=[pl.BlockSpec((1, tl, d), lambda b, t: (b, t, 0)),
                  pl.BlockSpec((1, 6, d), lambda b, t: (b, 0, 0))] + y_specs + w_specs,
        out_specs=pl.BlockSpec((1, tl, d), lambda b, t: (b, t, 0)),
        out_shape=jax.ShapeDtypeStruct((g, l, d), F32),
        compiler_params=_cparams("parallel", "parallel"),
        name="out_proj",
    )(x, mod, *ys, *ws)


def _ffn_kernel(gi, x_ref, m_ref, h_ref, wg_ref, wu_ref, wd_ref, o_ref, acc_ref):
    f = pl.program_id(2)
    h = h_ref[0]
    a = _silu(_dot(h, wg_ref[...])) * _dot(h, wu_ref[...])
    part = _dot(a.astype(BF16), wd_ref[...])

    @pl.when(f == 0)
    def _():
        acc_ref[...] = part

    @pl.when(f > 0)
    def _():
        acc_ref[...] += part

    @pl.when(f == pl.num_programs(2) - 1)
    def _():
        o_ref[0] = x_ref[0] + m_ref[0, gi:gi + 1, :] * acc_ref[...]


def _ffn(x, mod, gi, h, wg, wu, wd):
    g, l, d = x.shape
    dff = wg.shape[1]
    tl = min(l, TOK_TILE)
    tf = 512
    return pl.pallas_call(
        functools.partial(_ffn_kernel, gi),
        grid=(g, l // tl, dff // tf),
        in_specs=[pl.BlockSpec((1, tl, d), lambda b, t, f: (b, t, 0)),
                  pl.BlockSpec((1, 6, d), lambda b, t, f: (b, 0, 0)),
                  pl.BlockSpec((1, tl, d), lambda b, t, f: (b, t, 0)),
                  pl.BlockSpec((d, tf), lambda b, t, f: (0, f)),
                  pl.BlockSpec((d, tf), lambda b, t, f: (0, f)),
                  pl.BlockSpec((tf, d), lambda b, t, f: (f, 0))],
        out_specs=pl.BlockSpec((1, tl, d), lambda b, t, f: (b, t, 0)),
        out_shape=jax.ShapeDtypeStruct((g, l, d), F32),
        scratch_shapes=[pltpu.VMEM((tl, d), F32)],
        compiler_params=_cparams("parallel", "parallel", "arbitrary"),
        name="ffn",
    )(x, mod, h, wg, wu, wd)


def _router_kernel(h_ref, w_ref, o_ref):
    logits = _dot(h_ref[0], w_ref[...])
    lane = lax.broadcasted_iota(jnp.int32, logits.shape, 1).astype(F32)
    l1 = jnp.where(lane < N_EXPERTS, logits, -jnp.inf)
    m1 = jnp.max(l1, axis=-1, keepdims=True)
    i1 = jnp.min(jnp.where(l1 == m1, lane, float(LANES)), axis=-1, keepdims=True)
    l2 = jnp.where(lane == i1, -jnp.inf, l1)
    m2 = jnp.max(l2, axis=-1, keepdims=True)
    i2 = jnp.min(jnp.where(l2 == m2, lane, float(LANES)), axis=-1, keepdims=True)
    e2 = jnp.exp(m2 - m1)
    o_ref[0] = jnp.where(lane == i1, 1.0 / (1.0 + e2), jnp.where(lane == i2, e2 / (1.0 + e2), 0.0))


def _router(h, w_pad):
    g, l, d = h.shape
    tl = min(l, TOK_TILE)
    return pl.pallas_call(
        _router_kernel,
        grid=(g, l // tl),
        in_specs=[pl.BlockSpec((1, tl, d), lambda b, t: (b, t, 0)),
                  pl.BlockSpec((d, LANES), lambda b, t: (0, 0))],
        out_specs=pl.BlockSpec((1, tl, LANES), lambda b, t: (b, t, 0)),
        out_shape=jax.ShapeDtypeStruct((g, l, LANES), F32),
        compiler_params=_cparams("parallel", "parallel"),
        name="router",
    )(h, w_pad)


def _moe_kernel(gi, x_ref, m_ref, h_ref, gate_ref, wg_ref, wu_ref, wd_ref, o_ref, acc_ref):
    e = pl.program_id(2)
    f = pl.program_id(3)
    h = h_ref[0]
    gate = gate_ref[0]
    lane = lax.broadcasted_iota(jnp.int32, gate.shape, 1)
    ge = jnp.sum(jnp.where(lane == e, gate, 0.0), axis=-1, keepdims=True)
    a = _silu(_dot(h, wg_ref[0])) * _dot(h, wu_ref[0]) * ge
    part = _dot(a.astype(BF16), wd_ref[0])
    first = jnp.logical_and(e == 0, f == 0)
    last = jnp.logical_and(e == pl.num_programs(2) - 1, f == pl.num_programs(3) - 1)

    @pl.when(first)
    def _():
        acc_ref[...] = part

    @pl.when(jnp.logical_not(first))
    def _():
        acc_ref[...] += part

    @pl.when(last)
    def _():
        o_ref[0] = x_ref[0] + m_ref[0, gi:gi + 1, :] * acc_ref[...]


def _moe(x, mod, gi, h, gate, wg, wu, wd):
    g, l, d = x.shape
    ne, _, dff = wg.shape
    tl = min(l, TOK_TILE)
    tf = 512
    return pl.pallas_call(
        functools.partial(_moe_kernel, gi),
        grid=(g, l // tl, ne, dff // tf),
        in_specs=[pl.BlockSpec((1, tl, d), lambda b, t, e, f: (b, t, 0)),
                  pl.BlockSpec((1, 6, d), lambda b, t, e, f: (b, 0, 0)),
                  pl.BlockSpec((1, tl, d), lambda b, t, e, f: (b, t, 0)),
                  pl.BlockSpec((1, tl, LANES), lambda b, t, e, f: (b, t, 0)),
                  pl.BlockSpec((1, d, tf), lambda b, t, e, f: (e, 0, f)),
                  pl.BlockSpec((1, d, tf), lambda b, t, e, f: (e, 0, f)),
                  pl.BlockSpec((1, tf, d), lambda b, t, e, f: (e, f, 0))],
        out_specs=pl.BlockSpec((1, tl, d), lambda b, t, e, f: (b, t, 0)),
        out_shape=jax.ShapeDtypeStruct((g, l, d), F32),
        scratch_shapes=[pltpu.VMEM((tl, d), F32)],
        compiler_params=_cparams("parallel", "parallel", "arbitrary", "arbitrary"),
        name="moe",
    )(x, mod, h, gate, wg, wu, wd)


def _na_bias_table(rpb, rows):
    w = GRID_W
    col = np.arange(w)
    c0 = np.clip(col - NA_WIN_C // 2, 0, w - NA_WIN_C)
    col_in = (col[None, :] >= c0[:, None]) & (col[None, :] < c0[:, None] + NA_WIN_C)
    col_idx = np.clip(col[None, :] - col[:, None], 1 - NA_WIN_C, NA_WIN_C - 1) + NA_WIN_C - 1
    n_groups = rows // NA_Q_ROWS
    onehot = (col_idx[None] == np.arange(2 * NA_WIN_C - 1)[:, None, None]).astype(np.float32)
    blocks = jnp.einsum('hdc,cqk->hdqk', rpb.astype(F32), jnp.asarray(onehot), precision=lax.Precision.HIGHEST)
    blocks = jnp.where(col_in[None, None], blocks, MASK_VALUE)
    masked = jnp.full_like(blocks[:, 0], MASK_VALUE)
    variants = []
    for grp in (0, 1, n_groups - 1):
        r = grp * NA_Q_ROWS
        start = _na_key_start(r, rows)
        q_rows = []
        for a in range(NA_Q_ROWS):
            ws = int(np.clip(r + a - NA_WIN_R // 2, 0, rows - NA_WIN_R))
            row = []
            for i in range(NA_K_ROWS):
                kr = start + i
                row.append(blocks[:, kr - (r + a) + NA_WIN_R - 1] if ws <= kr < ws + NA_WIN_R else masked)
            q_rows.append(jnp.concatenate(row, axis=-1))
        variants.append(jnp.concatenate(q_rows, axis=-2))
    return jnp.stack(variants, axis=1)


def _na_key_start(r, rows):
    return int(np.clip(r - NA_WIN_R // 2, 0, rows - NA_K_ROWS))


def _na_kernel(ctx_out, q_ref, k_ref, v_ref, qc_ref, kc_ref, vc_ref, bias_ref, qw_ref, kw_ref, *rest):
    if ctx_out:
        o_ref, oc_ref, kn_ref = rest
    else:
        o_ref, kn_ref = rest
    n_ctx = kc_ref.shape[1]
    seq = k_ref.shape[1]
    rows = seq // GRID_W
    n_groups = rows // NA_Q_ROWS
    nq = NA_Q_ROWS * GRID_W
    nk = NA_K_ROWS * GRID_W
    scale = HEAD_DIM ** -0.5
    qw = qw_ref[...]
    kw = kw_ref[...]
    kn_ref[0:n_ctx, :] = _rms(kc_ref[0], kw).astype(BF16)

    def norm_keys(j, carry):
        r0 = pl.multiple_of(j * nq, nq)
        kn_ref[pl.ds(n_ctx + r0, nq), :] = _rms(k_ref[0, pl.ds(r0, nq), :], kw).astype(BF16)
        return carry

    lax.fori_loop(0, seq // nq, norm_keys, 0)
    k_ctx = kn_ref[0:n_ctx, :]
    v_ctx = vc_ref[0]

    def group(g, carry):
        q0 = pl.multiple_of(g * nq, nq)
        k0 = pl.multiple_of(jnp.clip(g * NA_Q_ROWS - NA_WIN_R // 2, 0, rows - NA_K_ROWS) * GRID_W, GRID_W)
        variant = jnp.where(g == 0, 0, jnp.where(g == n_groups - 1, 2, 1))
        qn = (_rms(q_ref[0, pl.ds(q0, nq), :], qw) * scale).astype(BF16)
        s = _dot_nt(qn, kn_ref[pl.ds(n_ctx + k0, nk), :]) + bias_ref[0, variant]
        sc = _dot_nt(qn, k_ctx)
        m = jnp.maximum(jnp.max(s, axis=-1, keepdims=True), jnp.max(sc, axis=-1, keepdims=True))
        p = jnp.exp(s - m)
        pc = jnp.exp(sc - m)
        denom = jnp.sum(p, axis=-1, keepdims=True) + jnp.sum(pc, axis=-1, keepdims=True)
        o = _dot(p.astype(BF16), v_ref[0, pl.ds(k0, nk), :]) + _dot(pc.astype(BF16), v_ctx)
        o_ref[0, pl.ds(q0, nq), :] = (o / denom).astype(o_ref.dtype)
        return carry

    lax.fori_loop(0, n_groups, group, 0)

    if ctx_out:
        qn = (_rms(qc_ref[0], qw) * scale).astype(BF16)
        sc = _dot_nt(qn, k_ctx)
        pc = jnp.exp(sc - jnp.max(sc, axis=-1, keepdims=True))
        o = _dot(pc.astype(BF16), v_ctx) / jnp.sum(pc, axis=-1, keepdims=True)
        oc_ref[0] = o.astype(oc_ref.dtype)


def _na(proj_l, proj_c, bias, q_norm, k_norm, ctx_out):
    bsz, seq, _ = proj_l.shape
    n_ctx = proj_c.shape[1]
    h_ = NA_HEADS
    hd = HEAD_DIM
    lat = lambda off: pl.BlockSpec((1, seq, hd), lambda h, b: (b, 0, off + h))
    ctx = lambda off: pl.BlockSpec((1, n_ctx, hd), lambda h, b: (b, 0, off + h))
    out_shape = [jax.ShapeDtypeStruct((bsz, seq, h_ * hd), BF16)]
    out_specs = [pl.BlockSpec((1, seq, hd), lambda h, b: (b, 0, h))]
    if ctx_out:
        out_shape.append(jax.ShapeDtypeStruct((bsz, n_ctx, h_ * hd), BF16))
        out_specs.append(pl.BlockSpec((1, n_ctx, hd), lambda h, b: (b, 0, h)))
    res = pl.pallas_call(
        functools.partial(_na_kernel, ctx_out),
        grid=(h_, bsz),
        in_specs=[lat(0), lat(h_), lat(2 * h_), ctx(0), ctx(h_), ctx(2 * h_),
                  pl.BlockSpec((1,) + bias.shape[1:], lambda h, b: (h, 0, 0, 0)),
                  pl.BlockSpec((1, hd), lambda h, b: (0, 0)),
                  pl.BlockSpec((1, hd), lambda h, b: (0, 0))],
        out_specs=out_specs,
        out_shape=out_shape,
        scratch_shapes=[pltpu.VMEM((n_ctx + seq, hd), BF16)],
        compiler_params=_cparams("parallel", "parallel"),
        name="neighbourhood_attention",
    )(proj_l, proj_l, proj_l, proj_c, proj_c, proj_c, bias, q_norm.reshape(1, hd), k_norm.reshape(1, hd))
    return (res[0], res[1]) if ctx_out else (res[0], None)


def _pool_bands():
    t = SEQ_TILE
    i = np.arange(t)[:, None]
    j = np.arange(t)[None, :]
    bands = np.zeros((len(POOL_WINDOWS), 3, t, t), np.float32)
    for gi, win in enumerate(POOL_WINDOWS):
        half = win // 2
        for s, off in enumerate((-t, 0, t)):
            jj = j + off
            bands[gi, s] = (jj >= i - half) & (jj <= i + half - 1)
    return jnp.asarray(bands, BF16)


def _pool_kernel(seq_len, u_ref, up_ref, un_ref, band_ref, pw_ref, sc_ref, o_ref):
    t = pl.program_id(1)
    nt = pl.num_programs(1)
    tl = u_ref.shape[1]
    half = lax.shift_left(1, pl.program_id(2))
    has_prev = (t > 0).astype(F32)
    has_next = (t < nt - 1).astype(F32)
    pos = t * tl + lax.broadcasted_iota(jnp.int32, (tl, 1), 0)
    cur = u_ref[0]
    wsum = (_dot(band_ref[0, 1], cur) + has_prev * _dot(band_ref[0, 0], up_ref[0])
            + has_next * _dot(band_ref[0, 2], un_ref[0]))
    cnt = jnp.minimum(pos + half, seq_len) - jnp.maximum(pos - half, 0)
    d = wsum / cnt.astype(F32) - cur.astype(F32)
    o_ref[0] = (_dot(d.astype(BF16), pw_ref[0]) * sc_ref[...]).astype(o_ref.dtype)


def _pool(proj, bands, pool_w, pool_scale):
    assert POOL_WINDOWS == tuple(2 << g for g in range(len(POOL_WINDOWS)))
    bsz, l, _ = proj.shape
    tl = SEQ_TILE
    nt = l // tl
    ng = len(POOL_WINDOWS)
    cb = POOL_COL // LANES
    return pl.pallas_call(
        functools.partial(_pool_kernel, l),
        grid=(bsz, nt, ng),
        in_specs=[pl.BlockSpec((1, tl, LANES), lambda b, t, g: (b, t, cb + g)),
                  pl.BlockSpec((1, tl, LANES), lambda b, t, g: (b, jnp.maximum(t - 1, 0), cb + g)),
                  pl.BlockSpec((1, tl, LANES), lambda b, t, g: (b, jnp.minimum(t + 1, nt - 1), cb + g)),
                  pl.BlockSpec((1,) + bands.shape[1:], lambda b, t, g: (g, 0, 0, 0)),
                  pl.BlockSpec((1, LANES, LANES), lambda b, t, g: (g, 0, 0)),
                  pl.BlockSpec((1, LANES), lambda b, t, g: (0, g))],
        out_specs=pl.BlockSpec((1, tl, LANES), lambda b, t, g: (b, t, g)),
        out_shape=jax.ShapeDtypeStruct((bsz, l, ng * LANES), BF16),
        compiler_params=_cparams("parallel", "parallel", "parallel"),
        name="pool_mixer",
    )(proj, proj, proj, bands, pool_w, pool_scale.reshape(1, ng * LANES))


def _gdn_prep_kernel(x_ref, xp_ref, xn_ref, w_ref, o_ref):
    t = pl.program_id(1)
    j = pl.program_id(2)
    tl = x_ref.shape[1]
    halo = xp_ref.shape[1]
    x = x_ref[0].astype(F32)
    row = lax.broadcasted_iota(jnp.int32, x.shape, 0)
    prev_row = jnp.where(t > 0, xp_ref[0, halo - 1:halo, :].astype(F32), 0.0)
    next_row = jnp.where(t < pl.num_programs(1) - 1, xn_ref[0, 0:1, :].astype(F32), 0.0)
    x_m1 = jnp.where(row == 0, prev_row, pltpu.roll(x, 1, axis=0))
    x_p1 = jnp.where(row == tl - 1, next_row, pltpu.roll(x, tl - 1, axis=0))
    y = _silu(w_ref[0:1, :] * x_m1 + w_ref[1:2, :] * x + w_ref[2:3, :] * x_p1)
    third = pl.num_programs(2) // 3
    for hh in range(y.shape[1] // HEAD_DIM):
        cs = slice(hh * HEAD_DIM, (hh + 1) * HEAD_DIM)
        yh = y[:, cs]
        inv = lax.rsqrt(jnp.sum(yh * yh, axis=-1, keepdims=True) + EPS)
        fac = jnp.where(j < third, inv * HEAD_DIM ** -0.5, jnp.where(j < 2 * third, inv, 1.0))
        o_ref[0, :, cs] = (yh * fac).astype(o_ref.dtype)


def _gdn_prep(proj, conv_w):
    bsz, l, _ = proj.shape
    width = conv_w.shape[1]
    tl = min(l, 4 * SEQ_TILE)
    tc = 2 * HEAD_DIM
    halo = 16
    nt = l // tl
    cb = GDN_COL // tc
    nh = l // halo
    return pl.pallas_call(
        _gdn_prep_kernel,
        grid=(bsz, nt, width // tc),
        in_specs=[pl.BlockSpec((1, tl, tc), lambda b, t, j: (b, t, cb + j)),
                  pl.BlockSpec((1, halo, tc), lambda b, t, j: (b, jnp.maximum(t * (tl // halo) - 1, 0), cb + j)),
                  pl.BlockSpec((1, halo, tc), lambda b, t, j: (b, jnp.minimum((t + 1) * (tl // halo), nh - 1), cb + j)),
                  pl.BlockSpec((3, tc), lambda b, t, j: (0, j))],
        out_specs=pl.BlockSpec((1, tl, tc), lambda b, t, j: (b, t, j)),
        out_shape=jax.ShapeDtypeStruct((bsz, l, width), BF16),
        compiler_params=_cparams("parallel", "parallel", "parallel"),
        name="gdn_prep",
    )(proj, proj, proj, conv_w)


def _gdn_gates_kernel(ab_ref, neg_a_ref, dt_ref, is_a_ref, is_bwd_ref, o_ref):
    x = ab_ref[0]
    tl = x.shape[0]
    z = x + dt_ref[...]
    g = neg_a_ref[...] * (jnp.maximum(z, 0.0) + jnp.log(1.0 + jnp.exp(-jnp.abs(z))))
    beta = 1.0 / (1.0 + jnp.exp(-x))
    pos = lax.broadcasted_iota(jnp.int32, x.shape, 0) % GDN_CHUNK
    fwd = g
    bwd = g
    s = 1
    while s < GDN_CHUNK:
        fwd = fwd + jnp.where(pos >= s, pltpu.roll(fwd, s, axis=0), 0.0)
        bwd = bwd + jnp.where(pos < GDN_CHUNK - s, pltpu.roll(bwd, tl - s, axis=0), 0.0)
        s *= 2
    gc = jnp.where(is_bwd_ref[...] > 0, bwd, fwd)
    o_ref[0] = jnp.where(is_a_ref[...] > 0, gc, beta)


def _gdn_gates(ab, a_log, dt_bias):
    bsz, l, _ = ab.shape
    nh = GDN_HEADS
    neg_a = jnp.zeros((2, 2, nh), F32).at[:, 0].set(-jnp.exp(a_log.astype(F32)))
    dt = jnp.zeros((2, 2, nh), F32).at[:, 0].set(dt_bias.astype(F32))
    is_a = jnp.zeros((2, 2, nh), F32).at[:, 0].set(1.0)
    is_bwd = jnp.zeros((2, 2, nh), F32).at[1].set(1.0)
    lanes = lambda v: jnp.pad(v.reshape(1, 4 * nh), ((0, 0), (0, LANES - 4 * nh)))
    tl = SEQ_TILE
    vec = pl.BlockSpec((1, LANES), lambda b, t: (0, 0))
    return pl.pallas_call(
        _gdn_gates_kernel,
        grid=(bsz, l // tl),
        in_specs=[pl.BlockSpec((1, tl, LANES), lambda b, t: (b, t, 0)), vec, vec, vec, vec],
        out_specs=pl.BlockSpec((1, tl, LANES), lambda b, t: (b, t, 0)),
        out_shape=jax.ShapeDtypeStruct((bsz, l, LANES), F32),
        compiler_params=_cparams("parallel", "parallel"),
        name="gdn_gates",
    )(ab, lanes(neg_a), lanes(dt), lanes(is_a), lanes(is_bwd))


def _gdn_local_kernel(q_ref, k_ref, v_ref, gcol_ref, grow_ref, uw_f_ref, qk_f_ref, at_f_ref,
                      uw_b_ref, qk_b_ref, at_b_ref):
    cs = GDN_CHUNK
    tl = q_ref.shape[1]
    ii = lax.broadcasted_iota(jnp.int32, (tl, tl), 0)
    jj = lax.broadcasted_iota(jnp.int32, (tl, tl), 1)
    same = (ii // cs) == (jj // cs)
    outs = ((uw_f_ref, qk_f_ref, at_f_ref), (uw_b_ref, qk_b_ref, at_b_ref))
    q = q_ref[0].astype(F32)
    k = k_ref[0].astype(F32)
    v = v_ref[0].astype(F32)
    k16 = k_ref[0]
    qk = _dot_nt(q_ref[0], k16)
    kk = _dot_nt(k16, k16)
    for d in range(2):
        uw_ref, qkd_ref, at_ref = outs[d]
        gc = gcol_ref[0, 0, :, 2 * d:2 * d + 1]
        beta = gcol_ref[0, 0, :, 2 * d + 1:2 * d + 2]
        gr = grow_ref[0, 0, 2 * d:2 * d + 1, :]
        incl = jnp.logical_and(same, (ii >= jj) if d == 0 else (ii <= jj))
        strict = jnp.logical_and(same, (ii > jj) if d == 0 else (ii < jj))
        decay = jnp.where(incl, jnp.exp(jnp.where(incl, gc - gr, 0.0)), 0.0)
        pw = jnp.where(strict, kk * decay, 0.0) * (-beta)
        e_gc = jnp.exp(gc)
        x = jnp.concatenate([v * beta, k * (beta * e_gc)], axis=-1)
        x = x + _dot(pw.astype(BF16), x.astype(BF16))
        s = 2
        while s < cs:
            p16 = pw.astype(BF16)
            pw = _dot(p16, p16)
            x = x + _dot(pw.astype(BF16), x.astype(BF16))
            s *= 2
        uw_ref[0, 0] = x.astype(uw_ref.dtype)
        edge = jnp.logical_and(same, (jj % cs) == (cs - 1 if d == 0 else 0))
        g_last = jnp.sum(jnp.where(edge, gr, 0.0), axis=-1, keepdims=True)
        qkd_ref[0, 0] = jnp.concatenate([q * e_gc, k * jnp.exp(g_last - gc)], axis=-1).astype(qkd_ref.dtype)
        attn = qk * decay
        for c in range(tl // cs):
            rs = slice(c * cs, (c + 1) * cs)
            at_ref[0, 0, rs, :] = attn[rs, rs].astype(at_ref.dtype)


def _gdn_local(qkv, gcol, grow):
    bsz, l, _ = qkv.shape
    nh = GDN_HEADS
    hd = HEAD_DIM
    tl = SEQ_TILE
    col = lambda off: pl.BlockSpec((1, tl, hd), lambda b, h, t: (b, t, off + h))
    wide = pl.BlockSpec((1, 1, tl, 2 * hd), lambda b, h, t: (b, h, t, 0))
    att = pl.BlockSpec((1, 1, tl, GDN_CHUNK), lambda b, h, t: (b, h, t, 0))
    s_wide = jax.ShapeDtypeStruct((bsz, nh, l, 2 * hd), BF16)
    s_att = jax.ShapeDtypeStruct((bsz, nh, l, GDN_CHUNK), BF16)
    return pl.pallas_call(
        _gdn_local_kernel,
        grid=(bsz, nh, l // tl),
        in_specs=[col(0), col(nh), col(2 * nh),
                  pl.BlockSpec((1, 1, tl, 4), lambda b, h, t: (b, h, t, 0)),
                  pl.BlockSpec((1, 1, 4, tl), lambda b, h, t: (b, h, 0, t))],
        out_specs=[wide, wide, att, wide, wide, att],
        out_shape=[s_wide, s_wide, s_att, s_wide, s_wide, s_att],
        compiler_params=_cparams("parallel", "parallel", "parallel"),
        name="gdn_local",
    )(qkv, qkv, qkv, gcol, grow)


def _gdn_scan_kernel(ctx_out, *refs):
    lat = refs[0:8]
    ctx = refs[8:16]
    nw_ref = refs[16]
    if ctx_out:
        o_ref, oc_ref, of_ref, ob_ref, ocf_ref, ocb_ref = refs[17:]
    else:
        o_ref, of_ref, ob_ref = refs[17:]
        oc_ref = ocf_ref = ocb_ref = None
    cs = GDN_CHUNK
    hd = HEAD_DIM

    def scan(seq_refs, out_f, out_b, state):
        uw_f, qk_f, at_f, uw_b, qk_b, at_b, gcol, _ = seq_refs
        n = uw_f.shape[2] // cs

        def one(uw, qk, at, r0, g_row, g_lane, s, out):
            s16 = s.astype(BF16)
            v_new = uw[0, 0, pl.ds(r0, cs), 0:hd].astype(F32) - _dot(uw[0, 0, pl.ds(r0, cs), hd:2 * hd], s16)
            vn16 = v_new.astype(BF16)
            if out is not None:
                out[pl.ds(r0, cs), :] = (_dot(qk[0, 0, pl.ds(r0, cs), 0:hd], s16)
                                         + _dot(at[0, 0, pl.ds(r0, cs), :], vn16))
            e_last = jnp.exp(gcol[0, 0, pl.ds(g_row, 1), g_lane:g_lane + 1])
            return s * e_last + _dot_tn(qk[0, 0, pl.ds(r0, cs), hd:2 * hd], vn16)

        def step(i, st):
            s_f, s_b = st
            rf = pl.multiple_of(i * cs, cs)
            rb = pl.multiple_of((n - 1 - i) * cs, cs)
            s_f = one(uw_f, qk_f, at_f, rf, rf + cs - 1, 0, s_f, out_f)
            s_b = one(uw_b, qk_b, at_b, rb, rb, 2, s_b, out_b)
            return s_f, s_b

        return lax.fori_loop(0, n, step, state)

    def finish(seq_refs, out_f, out_b, dst):
        z = seq_refs[7][0].astype(F32)
        o = _rms(out_f[...] + out_b[...], nw_ref[...])
        dst[0] = (o * _silu(z)).astype(dst.dtype)

    zero = jnp.zeros((hd, hd), F32)
    state = scan(ctx, ocf_ref, ocb_ref, (zero, zero))
    scan(lat, of_ref, ob_ref, state)
    finish(lat, of_ref, ob_ref, o_ref)
    if ctx_out:
        finish(ctx, ocf_ref, ocb_ref, oc_ref)


def _gdn_scan(loc_l, gcol_l, proj_l, loc_c, gcol_c, proj_c, norm_w, ctx_out):
    bsz, nh, seq, _ = loc_l[0].shape
    n_ctx = loc_c[0].shape[2]
    hd = HEAD_DIM
    z_col = Z_COL // hd

    def specs(l):
        wide = pl.BlockSpec((1, 1, l, 2 * hd), lambda b, h: (b, h, 0, 0))
        att = pl.BlockSpec((1, 1, l, GDN_CHUNK), lambda b, h: (b, h, 0, 0))
        return [wide, wide, att, wide, wide, att,
                pl.BlockSpec((1, 1, l, 4), lambda b, h: (b, h, 0, 0)),
                pl.BlockSpec((1, l, hd), lambda b, h: (b, 0, z_col + h))]

    out_shape = [jax.ShapeDtypeStruct((bsz, seq, nh * hd), BF16)]
    out_specs = [pl.BlockSpec((1, seq, hd), lambda b, h: (b, 0, h))]
    scratch = [pltpu.VMEM((seq, hd), F32), pltpu.VMEM((seq, hd), F32)]
    if ctx_out:
        out_shape.append(jax.ShapeDtypeStruct((bsz, n_ctx, nh * hd), BF16))
        out_specs.append(pl.BlockSpec((1, n_ctx, hd), lambda b, h: (b, 0, h)))
        scratch += [pltpu.VMEM((n_ctx, hd), F32), pltpu.VMEM((n_ctx, hd), F32)]
    res = pl.pallas_call(
        functools.partial(_gdn_scan_kernel, ctx_out),
        grid=(bsz, nh),
        in_specs=specs(seq) + specs(n_ctx) + [pl.BlockSpec((1, hd), lambda b, h: (0, 0))],
        out_specs=out_specs,
        out_shape=out_shape,
        scratch_shapes=scratch,
        compiler_params=_cparams("parallel", "parallel"),
        name="gdn_scan",
    )(*loc_l, gcol_l, proj_l, *loc_c, gcol_c, proj_c, norm_w.reshape(1, hd))
    return (res[0], res[1]) if ctx_out else (res[0], None)


def _gate_layouts(gb):
    bsz, l, _ = gb.shape
    nh = GDN_HEADS
    t = gb[:, :, :4 * nh].reshape(bsz, l, 2, 2, nh)
    gcol = jnp.transpose(t, (0, 4, 1, 2, 3)).reshape(bsz, nh, l, 4)
    grow = jnp.transpose(t, (0, 4, 2, 3, 1)).reshape(bsz, nh, 4, l)
    return gcol, grow


NA_WIDTH = NA_HEADS * HEAD_DIM
POOL_WIDTH = LANES * len(POOL_WINDOWS)
GDN_WIDTH = GDN_HEADS * HEAD_DIM
POOL_COL = 3 * NA_WIDTH
GDN_COL = POOL_COL + POOL_WIDTH
Z_COL = GDN_COL + 3 * GDN_WIDTH
AB_COL = Z_COL + GDN_WIDTH


def _token_mixers(proj_l, ab_l, proj_c, ab_c, lw, ctx_out):
    y_na_l, y_na_c = _na(proj_l, proj_c, lw["na_bias"], lw["na_q_norm"], lw["na_k_norm"], ctx_out)
    y_pool_l = _pool(proj_l, lw["bands"], lw["pool_w"], lw["pool_scale"])
    y_pool_c = _pool(proj_c, lw["bands"], lw["pool_w"], lw["pool_scale"]) if ctx_out else None

    def local(proj, ab):
        gb = _gdn_gates(ab, lw["gdn_a_log"], lw["gdn_dt_bias"])
        gcol, grow = _gate_layouts(gb)
        return _gdn_local(_gdn_prep(proj, lw["gdn_conv"]), gcol, grow), gcol

    loc_l, gcol_l = local(proj_l, ab_l)
    loc_c, gcol_c = local(proj_c, ab_c)
    y_gdn_l, y_gdn_c = _gdn_scan(loc_l, gcol_l, proj_l, loc_c, gcol_c, proj_c, lw["gdn_norm"], ctx_out)
    return (y_na_l, y_pool_l, y_gdn_l), (y_na_c, y_pool_c, y_gdn_c)


def kernel(x, c, ctx, c_ctx, w_mod, b_mod, w_in, w_out, na_q_norm, na_k_norm, na_rpb, pool_w, pool_scale,
           gdn_conv, gdn_a_log, gdn_dt_bias, gdn_norm, ffn_w_gate, ffn_w_up, ffn_w_down,
           moe_router, moe_w_gate, moe_w_up, moe_w_down):
    bsz, seq, d = x.shape
    n_ctx = ctx.shape[1]
    depth = w_mod.shape[0]
    assert seq % (GRID_W * NA_Q_ROWS) == 0 and seq // GRID_W >= NA_K_ROWS
    assert seq % SEQ_TILE == 0 and n_ctx % SEQ_TILE == 0 and w_in.shape[2] == AB_COL + 4 * GDN_HEADS

    mod_rows = 16
    c_rows = jnp.zeros((mod_rows, d), F32).at[:bsz].set(c).at[bsz].set(c_ctx)
    mods = _modulation(c_rows, w_mod, b_mod).reshape(depth, mod_rows, 6, d)
    bands = _pool_bands()

    xl = x
    xc = ctx.reshape(1, bsz * n_ctx, d)
    for i in range(depth):
        last = i == depth - 1
        mod_l = mods[i, :bsz]
        mod_c = mods[i, bsz:bsz + 1]
        w_main = w_in[i, :, :AB_COL].astype(BF16)
        w_ab = jnp.pad(w_in[i, :, AB_COL:], ((0, 0), (0, LANES - 4 * GDN_HEADS))).astype(BF16)
        wo = w_out[i].astype(BF16)
        ws = (wo[:NA_WIDTH], wo[NA_WIDTH:NA_WIDTH + POOL_WIDTH], wo[NA_WIDTH + POOL_WIDTH:])
        lw = dict(na_bias=_na_bias_table(na_rpb[i], seq // GRID_W), na_q_norm=na_q_norm[i], na_k_norm=na_k_norm[i],
                  bands=bands, pool_w=pool_w[i].astype(BF16), pool_scale=pool_scale[i], gdn_conv=gdn_conv[i],
                  gdn_a_log=gdn_a_log[i], gdn_dt_bias=gdn_dt_bias[i], gdn_norm=gdn_norm[i])

        h_l = _norm_mod(xl, mod_l, 0).reshape(bsz * seq, d)
        h_c = _norm_mod(xc, mod_c, 0).reshape(bsz * n_ctx, d)
        half = AB_COL // 2
        proj_l = _mm(h_l, w_main, BF16, TOK_TILE, half).reshape(bsz, seq, AB_COL)
        proj_c = _mm(h_c, w_main, BF16, TOK_TILE, half).reshape(bsz, n_ctx, AB_COL)
        ab_l = _mm(h_l, w_ab, F32, 2048, LANES).reshape(bsz, seq, LANES)
        ab_c = _mm(h_c, w_ab, F32, 2048, LANES).reshape(bsz, n_ctx, LANES)
        ys_l, ys_c = _token_mixers(proj_l, ab_l, proj_c, ab_c, lw, not last)
        xl = _out_proj(xl, mod_l, 2, ys_l, ws)
        if not last:
            xc = _out_proj(xc, mod_c, 2, [y.reshape(1, bsz * n_ctx, -1) for y in ys_c], ws)

        streams = [(xl, mod_l)] if last else [(xl, mod_l), (xc, mod_c)]
        outs = []
        for xs, mod in streams:
            h = _norm_mod(xs, mod, 3)
            if i % 2 == 0:
                j = i // 2
                outs.append(_ffn(xs, mod, 5, h, ffn_w_gate[j].astype(BF16), ffn_w_up[j].astype(BF16),
                                 ffn_w_down[j].astype(BF16)))
            else:
                j = i // 2
                rw = jnp.pad(moe_router[j], ((0, 0), (0, LANES - N_EXPERTS))).astype(BF16)
                gate = _router(h, rw)
                outs.append(_moe(xs, mod, 5, h, gate, moe_w_gate[j].astype(BF16), moe_w_up[j].astype(BF16),
                                 moe_w_down[j].astype(BF16)))
        xl = outs[0]
        if not last:
            xc = outs[1]
    return xl
```

```python
import functools

import numpy as np
import jax
import jax.numpy as jnp
from jax import lax
from jax.experimental import pallas as pl
from jax.experimental.pallas import tpu as pltpu

F32 = jnp.float32
BF16 = jnp.bfloat16

EPS = 1e-6
HEAD_DIM = 128
GRID_W = 64
NA_HEADS = 6
NA_WIN_R = 8
NA_WIN_C = 16
NA_Q_ROWS = 4
NA_K_ROWS = NA_Q_ROWS + NA_WIN_R
POOL_WINDOWS = (2, 4, 8, 16)
GDN_HEADS = 6
GDN_CHUNK = 64
N_EXPERTS = 8
LANES = 128
MASK_VALUE = -1e30
VMEM_LIMIT = 56 * 1024 * 1024
SEQ_TILE = 256
TOK_TILE = 512
MOE_TILE = 512
GATHER_TILE = 256


def _cparams(*sem):
    return pltpu.CompilerParams(dimension_semantics=sem, vmem_limit_bytes=VMEM_LIMIT)


def _silu(x):
    return x / (1.0 + jnp.exp(-x))


def _dot(a, b):
    return jnp.dot(a, b, preferred_element_type=F32)


def _dot_nt(a, b):
    return lax.dot_general(a, b, (((1,), (1,)), ((), ())), preferred_element_type=F32)


def _dot_tn(a, b):
    return lax.dot_general(a, b, (((0,), (0,)), ((), ())), preferred_element_type=F32)


def _rms(x, w):
    xf = x.astype(F32)
    return xf * lax.rsqrt(jnp.mean(xf * xf, axis=-1, keepdims=True) + EPS) * w


def _mod_kernel(a_ref, w_ref, b_ref, o_ref):
    a = _silu(a_ref[...])
    o_ref[0] = _dot(a.astype(BF16), w_ref[0].astype(BF16)) + b_ref[0]


def _modulation(c_rows, w_mod, b_mod):
    depth, d, n = w_mod.shape
    r = c_rows.shape[0]
    tn = 1024
    return pl.pallas_call(
        _mod_kernel,
        grid=(depth, n // tn),
        in_specs=[pl.BlockSpec((r, d), lambda i, j: (0, 0)),
                  pl.BlockSpec((1, d, tn), lambda i, j: (i, 0, j)),
                  pl.BlockSpec((1, 1, tn), lambda i, j: (i, 0, j))],
        out_specs=pl.BlockSpec((1, r, tn), lambda i, j: (i, 0, j)),
        out_shape=jax.ShapeDtypeStruct((depth, r, n), F32),
        compiler_params=_cparams("parallel", "parallel"),
        name="modulation",
    )(c_rows, w_mod, b_mod.reshape(depth, 1, n))


HIGH_HALF = 0xFFFF0000


def _pack_bf16_pairs(h):
    half = h.shape[1] // 2
    hf = h.astype(F32)
    lo = lax.bitcast_convert_type(hf[:, :half], jnp.uint32) >> 16
    hi = lax.bitcast_convert_type(hf[:, half:], jnp.uint32) & jnp.uint32(HIGH_HALF)
    return hi | lo


def _unpack_bf16_pairs(w):
    lo = lax.bitcast_convert_type(w << 16, F32).astype(BF16)
    hi = lax.bitcast_convert_type(w & jnp.uint32(HIGH_HALF), F32).astype(BF16)
    return lo, hi


def _norm_mod_kernel(si, x_ref, m_ref, o_ref, *packed_ref):
    x = x_ref[0]
    y = x * lax.rsqrt(jnp.mean(x * x, axis=-1, keepdims=True) + EPS)
    h = (y * (1.0 + m_ref[0, si + 1:si + 2, :]) + m_ref[0, si:si + 1, :]).astype(BF16)
    o_ref[0] = h
    if packed_ref:
        packed_ref[0][0] = _pack_bf16_pairs(h)


def _norm_mod(x, mod, si, packed=False):
    g, l, d = x.shape
    tl = min(l, TOK_TILE)
    out_shape = [jax.ShapeDtypeStruct((g, l, d), BF16)]
    out_specs = [pl.BlockSpec((1, tl, d), lambda b, t: (b, t, 0))]
    if packed:
        out_shape.append(jax.ShapeDtypeStruct((g, l, d // 2), jnp.uint32))
        out_specs.append(pl.BlockSpec((1, tl, d // 2), lambda b, t: (b, t, 0)))
    res = pl.pallas_call(
        functools.partial(_norm_mod_kernel, si),
        grid=(g, l // tl),
        in_specs=[pl.BlockSpec((1, tl, d), lambda b, t: (b, t, 0)),
                  pl.BlockSpec((1, 6, d), lambda b, t: (b, 0, 0))],
        out_specs=out_specs,
        out_shape=out_shape,
        compiler_params=_cparams("parallel", "parallel"),
        name="norm_mod",
    )(x, mod)
    return tuple(res) if packed else res[0]


def _mm_kernel(a_ref, w_ref, o_ref):
    o_ref[...] = _dot(a_ref[...], w_ref[...]).astype(o_ref.dtype)


def _mm(a, w, out_dtype, tm, tn):
    m, k = a.shape
    n = w.shape[1]
    tm = min(tm, m)
    return pl.pallas_call(
        _mm_kernel,
        grid=(n // tn, m // tm),
        in_specs=[pl.BlockSpec((tm, k), lambda j, i: (i, 0)),
                  pl.BlockSpec((k, tn), lambda j, i: (0, j))],
        out_specs=pl.BlockSpec((tm, tn), lambda j, i: (i, j)),
        out_shape=jax.ShapeDtypeStruct((m, n), out_dtype),
        compiler_params=_cparams("parallel", "parallel"),
        name="matmul",
    )(a, w)


def _out_proj_kernel(gi, x_ref, m_ref, y0_ref, y1_ref, y2_ref, w0_ref, w1_ref, w2_ref, o_ref):
    acc = _dot(y0_ref[0], w0_ref[...]) + _dot(y1_ref[0], w1_ref[...]) + _dot(y2_ref[0], w2_ref[...])
    o_ref[0] = x_ref[0] + m_ref[0, gi:gi + 1, :] * acc


def _out_proj(x, mod, gi, ys, ws):
    g, l, d = x.shape
    tl = min(l, TOK_TILE)
    y_specs = [pl.BlockSpec((1, tl, y.shape[2]), lambda b, t: (b, t, 0)) for y in ys]
    w_specs = [pl.BlockSpec(w.shape, lambda b, t: (0, 0)) for w in ws]
    return pl.pallas_call(
        functools.partial(_out_proj_kernel, gi),
        grid=(g, l // tl),
        in_specs=[pl.BlockSpec((1, tl, d), lambda b, t: (b, t, 0)),
                  pl.BlockSpec((1, 6, d), lambda b, t: (b, 0, 0))] + y_specs + w_specs,
        out_specs=pl.BlockSpec((1, tl, d), lambda b, t: (b, t, 0)),
        out_shape=jax.ShapeDtypeStruct((g, l, d), F32),
        compiler_params=_cparams("parallel", "parallel"),
        name="out_proj",
    )(x, mod, *ys, *ws)


def _ffn_kernel(gi, x_ref, m_ref, h_ref, wg_ref, wu_ref, wd_ref, o_ref, acc_ref):
    f = pl.program_id(2)
    h = h_ref[0]
    a = _silu(_dot(h, wg_ref[...])) * _dot(h, wu_ref[...])
    part = _dot(a.astype(BF16), wd_ref[...])

    @pl.when(f == 0)
    def _():
        acc_ref[...] = part

    @pl.when(f > 0)
    def _():
        acc_ref[...] += part

    @pl.when(f == pl.num_programs(2) - 1)
    def _():
        o_ref[0] = x_ref[0] + m_ref[0, gi:gi + 1, :] * acc_ref[...]


def _ffn(x, mod, gi, h, wg, wu, wd):
    g, l, d = x.shape
    dff = wg.shape[1]
    tl = min(l, TOK_TILE)
    tf = 512
    return pl.pallas_call(
        functools.partial(_ffn_kernel, gi),
        grid=(g, l // tl, dff // tf),
        in_specs=[pl.BlockSpec((1, tl, d), lambda b, t, f: (b, t, 0)),
                  pl.BlockSpec((1, 6, d), lambda b, t, f: (b, 0, 0)),
                  pl.BlockSpec((1, tl, d), lambda b, t, f: (b, t, 0)),
                  pl.BlockSpec((d, tf), lambda b, t, f: (0, f)),
                  pl.BlockSpec((d, tf), lambda b, t, f: (0, f)),
                  pl.BlockSpec((tf, d), lambda b, t, f: (f, 0))],
        out_specs=pl.BlockSpec((1, tl, d), lambda b, t, f: (b, t, 0)),
        out_shape=jax.ShapeDtypeStruct((g, l, d), F32),
        scratch_shapes=[pltpu.VMEM((tl, d), F32)],
        compiler_params=_cparams("parallel", "parallel", "arbitrary"),
        name="ffn",
    )(x, mod, h, wg, wu, wd)


def _router_kernel(h_ref, w_ref, o_ref):
    logits = _dot(h_ref[0], w_ref[...])
    lane = lax.broadcasted_iota(jnp.int32, logits.shape, 1).astype(F32)
    l1 = jnp.where(lane < N_EXPERTS, logits, -jnp.inf)
    m1 = jnp.max(l1, axis=-1, keepdims=True)
    i1 = jnp.min(jnp.where(l1 == m1, lane, float(LANES)), axis=-1, keepdims=True)
    l2 = jnp.where(lane == i1, -jnp.inf, l1)
    m2 = jnp.max(l2, axis=-1, keepdims=True)
    i2 = jnp.min(jnp.where(l2 == m2, lane, float(LANES)), axis=-1, keepdims=True)
    e2 = jnp.exp(m2 - m1)
    w1 = 1.0 / (1.0 + e2)
    w2 = e2 / (1.0 + e2)
    o_ref[0] = jnp.where(lane == 0, i1, jnp.where(lane == 1, i2, jnp.where(lane == 2, w1, jnp.where(lane == 3, w2, 0.0))))


def _router(h, w_pad):
    g, l, d = h.shape
    tl = min(l, TOK_TILE)
    return pl.pallas_call(
        _router_kernel,
        grid=(g, l // tl),
        in_specs=[pl.BlockSpec((1, tl, d), lambda b, t: (b, t, 0)),
                  pl.BlockSpec((d, LANES), lambda b, t: (0, 0))],
        out_specs=pl.BlockSpec((1, tl, LANES), lambda b, t: (b, t, 0)),
        out_shape=jax.ShapeDtypeStruct((g, l, LANES), F32),
        compiler_params=_cparams("parallel", "parallel"),
        name="router",
    )(h, w_pad)


def _dispatch_plan(info):
    t = info.shape[0] * info.shape[1]
    flat = info.reshape(t, LANES)
    e = jnp.concatenate([flat[:, 0], flat[:, 1]]).astype(jnp.int32)
    onehot = (e[:, None] == jnp.arange(N_EXPERTS, dtype=jnp.int32)[None, :]).astype(jnp.int32)
    csum = jnp.cumsum(onehot, axis=0)
    rank = jnp.sum(csum * onehot, axis=1) - 1
    padded = (csum[-1] + MOE_TILE - 1) // MOE_TILE * MOE_TILE
    ends = jnp.cumsum(padded)
    pos = jnp.sum(onehot * (ends - padded)[None, :], axis=1) + rank
    n_rows = 2 * t + N_EXPERTS * MOE_TILE
    tok = jnp.tile(jnp.arange(t, dtype=jnp.int32), 2)
    row_token = jnp.zeros((n_rows,), jnp.int32).at[pos].set(tok)
    tile_start = jnp.arange(n_rows // MOE_TILE, dtype=jnp.int32) * MOE_TILE
    tile_expert = jnp.minimum(jnp.sum((ends[None, :] <= tile_start[:, None]).astype(jnp.int32), axis=1), N_EXPERTS - 1)
    n_used = (ends[-1] // MOE_TILE).astype(jnp.int32).reshape(1)
    return row_token, tile_expert, n_used, pos[:t], pos[t:]


def _row_copy(src_ref, dst_ref, sem, src_row, dst_row):
    return pltpu.make_async_copy(src_ref.at[pl.ds(src_row, 1)], dst_ref.at[pl.ds(dst_row, 1)], sem)


def _gather_kernel(idx_ref, src_ref, o_ref, sem):
    n = o_ref.shape[0]

    def issue(r, carry):
        _row_copy(src_ref, o_ref, sem, idx_ref[0, 0, r], r).start()
        return carry

    def wait(r, carry):
        _row_copy(src_ref, o_ref, sem, 0, 0).wait()
        return carry

    lax.fori_loop(0, n, issue, 0, unroll=8)
    lax.fori_loop(0, n, wait, 0, unroll=8)


def _gather_rows(src, idx):
    n = idx.shape[0]
    w = src.shape[1]
    tg = MOE_TILE
    return pl.pallas_call(
        _gather_kernel,
        grid=(n // tg,),
        in_specs=[pl.BlockSpec((1, 1, tg), lambda i: (i, 0, 0), memory_space=pltpu.SMEM),
                  pl.BlockSpec(memory_space=pl.ANY)],
        out_specs=pl.BlockSpec((tg, w), lambda i: (i, 0)),
        out_shape=jax.ShapeDtypeStruct((n, w), src.dtype),
        scratch_shapes=[pltpu.SemaphoreType.DMA(())],
        compiler_params=_cparams("arbitrary"),
        name="moe_gather",
    )(idx.reshape(n // tg, 1, tg), src)


def _experts_kernel(te_ref, nu_ref, hp_ref, wg_ref, wu_ref, wd_ref, o_ref, acc_ref):
    i = pl.program_id(0)
    f = pl.program_id(1)
    nf = pl.num_programs(1)
    used = i < nu_ref[0]

    @pl.when(used)
    def _():
        lo, hi = _unpack_bf16_pairs(hp_ref[...])
        half = lo.shape[1]
        g = _dot(lo, wg_ref[0, :half, :]) + _dot(hi, wg_ref[0, half:, :])
        u = _dot(lo, wu_ref[0, :half, :]) + _dot(hi, wu_ref[0, half:, :])
        part = _dot((_silu(g) * u).astype(BF16), wd_ref[0])

        @pl.when(f == 0)
        def _():
            acc_ref[...] = part

        @pl.when(f > 0)
        def _():
            acc_ref[...] += part

        @pl.when(f == nf - 1)
        def _():
            o_ref[...] = acc_ref[...]

    @pl.when(jnp.logical_and(jnp.logical_not(used), f == nf - 1))
    def _():
        o_ref[...] = jnp.zeros_like(o_ref)


def _experts(h_sorted, tile_expert, n_used, wg, wu, wd):
    n, half = h_sorted.shape
    d = 2 * half
    dff = wg.shape[2]
    tm = MOE_TILE
    tf = 512
    grid_spec = pltpu.PrefetchScalarGridSpec(
        num_scalar_prefetch=2,
        grid=(n // tm, dff // tf),
        in_specs=[pl.BlockSpec((tm, half), lambda i, f, te, nu: (i, 0)),
                  pl.BlockSpec((1, d, tf), lambda i, f, te, nu: (te[i], 0, f)),
                  pl.BlockSpec((1, d, tf), lambda i, f, te, nu: (te[i], 0, f)),
                  pl.BlockSpec((1, tf, d), lambda i, f, te, nu: (te[i], f, 0))],
        out_specs=pl.BlockSpec((tm, d), lambda i, f, te, nu: (i, 0)),
        scratch_shapes=[pltpu.VMEM((tm, d), F32)],
    )
    return pl.pallas_call(
        _experts_kernel,
        grid_spec=grid_spec,
        out_shape=jax.ShapeDtypeStruct((n, d), F32),
        compiler_params=_cparams("arbitrary", "arbitrary"),
        name="moe_experts",
    )(tile_expert, n_used, h_sorted, wg, wu, wd)


def _combine_kernel(gi, p1_ref, p2_ref, x_ref, m_ref, info_ref, y_ref, o_ref, a_ref, b_ref, sem):
    n = x_ref.shape[1]

    def issue(r, carry):
        _row_copy(y_ref, a_ref, sem.at[0], p1_ref[0, 0, r], r).start()
        _row_copy(y_ref, b_ref, sem.at[1], p2_ref[0, 0, r], r).start()
        return carry

    def wait(r, carry):
        _row_copy(y_ref, a_ref, sem.at[0], 0, 0).wait()
        _row_copy(y_ref, b_ref, sem.at[1], 0, 0).wait()
        return carry

    lax.fori_loop(0, n, issue, 0, unroll=8)
    lax.fori_loop(0, n, wait, 0, unroll=8)
    info = info_ref[0]
    mix = info[:, 2:3] * a_ref[...] + info[:, 3:4] * b_ref[...]
    o_ref[0] = x_ref[0] + m_ref[0, gi:gi + 1, :] * mix


def _combine(x, mod, gi, info, y_sorted, pos1, pos2):
    g, l, d = x.shape
    tc = GATHER_TILE
    nt = l // tc
    idx_spec = pl.BlockSpec((1, 1, tc), lambda b, t: (b * nt + t, 0, 0), memory_space=pltpu.SMEM)
    return pl.pallas_call(
        functools.partial(_combine_kernel, gi),
        grid=(g, nt),
        in_specs=[idx_spec, idx_spec,
                  pl.BlockSpec((1, tc, d), lambda b, t: (b, t, 0)),
                  pl.BlockSpec((1, 6, d), lambda b, t: (b, 0, 0)),
                  pl.BlockSpec((1, tc, LANES), lambda b, t: (b, t, 0)),
                  pl.BlockSpec(memory_space=pl.ANY)],
        out_specs=pl.BlockSpec((1, tc, d), lambda b, t: (b, t, 0)),
        out_shape=jax.ShapeDtypeStruct((g, l, d), F32),
        scratch_shapes=[pltpu.VMEM((tc, d), F32), pltpu.VMEM((tc, d), F32), pltpu.SemaphoreType.DMA((2,))],
        compiler_params=_cparams("arbitrary", "arbitrary"),
        name="moe_combine",
    )(pos1.reshape(g * nt, 1, tc), pos2.reshape(g * nt, 1, tc), x, mod, info, y_sorted)


def _moe(x, mod, gi, h, h_packed, router_w, wg, wu, wd):
    g, l, d = x.shape
    info = _router(h, router_w)
    row_token, tile_expert, n_used, pos1, pos2 = _dispatch_plan(info)
    h_sorted = _gather_rows(h_packed.reshape(g * l, d // 2), row_token)
    y_sorted = _experts(h_sorted, tile_expert, n_used, wg, wu, wd)
    return _combine(x, mod, gi, info, y_sorted, pos1, pos2)


def _na_bias_table(rpb, rows):
    w = GRID_W
    col = np.arange(w)
    c0 = np.clip(col - NA_WIN_C // 2, 0, w - NA_WIN_C)
    col_in = (col[None, :] >= c0[:, None]) & (col[None, :] < c0[:, None] + NA_WIN_C)
    col_idx = np.clip(col[None, :] - col[:, None], 1 - NA_WIN_C, NA_WIN_C - 1) + NA_WIN_C - 1
    n_groups = rows // NA_Q_ROWS
    onehot = (col_idx[None] == np.arange(2 * NA_WIN_C - 1)[:, None, None]).astype(np.float32)
    blocks = jnp.einsum('hdc,cqk->hdqk', rpb.astype(F32), jnp.asarray(onehot), precision=lax.Precision.HIGHEST)
    blocks = jnp.where(col_in[None, None], blocks, MASK_VALUE)
    masked = jnp.full_like(blocks[:, 0], MASK_VALUE)
    variants = []
    for grp in (0, 1, n_groups - 1):
        r = grp * NA_Q_ROWS
        start = _na_key_start(r, rows)
        q_rows = []
        for a in range(NA_Q_ROWS):
            ws = int(np.clip(r + a - NA_WIN_R // 2, 0, rows - NA_WIN_R))
            row = []
            for i in range(NA_K_ROWS):
                kr = start + i
                row.append(blocks[:, kr - (r + a) + NA_WIN_R - 1] if ws <= kr < ws + NA_WIN_R else masked)
            q_rows.append(jnp.concatenate(row, axis=-1))
        variants.append(jnp.concatenate(q_rows, axis=-2))
    return jnp.stack(variants, axis=1)


def _na_key_start(r, rows):
    return int(np.clip(r - NA_WIN_R // 2, 0, rows - NA_K_ROWS))


def _na_kernel(ctx_out, q_ref, k_ref, v_ref, qc_ref, kc_ref, vc_ref, bias_ref, qw_ref, kw_ref, *rest):
    if ctx_out:
        o_ref, oc_ref, kn_ref = rest
    else:
        o_ref, kn_ref = rest
    n_ctx = kc_ref.shape[1]
    seq = k_ref.shape[1]
    rows = seq // GRID_W
    n_groups = rows // NA_Q_ROWS
    nq = NA_Q_ROWS * GRID_W
    nk = NA_K_ROWS * GRID_W
    scale = HEAD_DIM ** -0.5
    qw = qw_ref[...]
    kw = kw_ref[...]
    kn_ref[0:n_ctx, :] = _rms(kc_ref[0], kw).astype(BF16)

    def norm_keys(j, carry):
        r0 = pl.multiple_of(j * nq, nq)
        kn_ref[pl.ds(n_ctx + r0, nq), :] = _rms(k_ref[0, pl.ds(r0, nq), :], kw).astype(BF16)
        return carry

    lax.fori_loop(0, seq // nq, norm_keys, 0)
    k_ctx = kn_ref[0:n_ctx, :]
    v_ctx = vc_ref[0]

    def group(g, carry):
        q0 = pl.multiple_of(g * nq, nq)
        k0 = pl.multiple_of(jnp.clip(g * NA_Q_ROWS - NA_WIN_R // 2, 0, rows - NA_K_ROWS) * GRID_W, GRID_W)
        variant = jnp.where(g == 0, 0, jnp.where(g == n_groups - 1, 2, 1))
        qn = (_rms(q_ref[0, pl.ds(q0, nq), :], qw) * scale).astype(BF16)
        s = _dot_nt(qn, kn_ref[pl.ds(n_ctx + k0, nk), :]) + bias_ref[0, variant]
        sc = _dot_nt(qn, k_ctx)
        m = jnp.maximum(jnp.max(s, axis=-1, keepdims=True), jnp.max(sc, axis=-1, keepdims=True))
        p = jnp.exp(s - m)
        pc = jnp.exp(sc - m)
        denom = jnp.sum(p, axis=-1, keepdims=True) + jnp.sum(pc, axis=-1, keepdims=True)
        o = _dot(p.astype(BF16), v_ref[0, pl.ds(k0, nk), :]) + _dot(pc.astype(BF16), v_ctx)
        o_ref[0, pl.ds(q0, nq), :] = (o / denom).astype(o_ref.dtype)
        return carry

    lax.fori_loop(0, n_groups, group, 0)

    if ctx_out:
        qn = (_rms(qc_ref[0], qw) * scale).astype(BF16)
        sc = _dot_nt(qn, k_ctx)
        pc = jnp.exp(sc - jnp.max(sc, axis=-1, keepdims=True))
        o = _dot(pc.astype(BF16), v_ctx) / jnp.sum(pc, axis=-1, keepdims=True)
        oc_ref[0] = o.astype(oc_ref.dtype)


def _na(proj_l, proj_c, bias, q_norm, k_norm, ctx_out):
    bsz, seq, _ = proj_l.shape
    n_ctx = proj_c.shape[1]
    h_ = NA_HEADS
    hd = HEAD_DIM
    lat = lambda off: pl.BlockSpec((1, seq, hd), lambda h, b: (b, 0, off + h))
    ctx = lambda off: pl.BlockSpec((1, n_ctx, hd), lambda h, b: (b, 0, off + h))
    out_shape = [jax.ShapeDtypeStruct((bsz, seq, h_ * hd), BF16)]
    out_specs = [pl.BlockSpec((1, seq, hd), lambda h, b: (b, 0, h))]
    if ctx_out:
        out_shape.append(jax.ShapeDtypeStruct((bsz, n_ctx, h_ * hd), BF16))
        out_specs.append(pl.BlockSpec((1, n_ctx, hd), lambda h, b: (b, 0, h)))
    res = pl.pallas_call(
        functools.partial(_na_kernel, ctx_out),
        grid=(h_, bsz),
        in_specs=[lat(0), lat(h_), lat(2 * h_), ctx(0), ctx(h_), ctx(2 * h_),
                  pl.BlockSpec((1,) + bias.shape[1:], lambda h, b: (h, 0, 0, 0)),
                  pl.BlockSpec((1, hd), lambda h, b: (0, 0)),
                  pl.BlockSpec((1, hd), lambda h, b: (0, 0))],
        out_specs=out_specs,
        out_shape=out_shape,
        scratch_shapes=[pltpu.VMEM((n_ctx + seq, hd), BF16)],
        compiler_params=_cparams("parallel", "parallel"),
        name="neighbourhood_attention",
    )(proj_l, proj_l, proj_l, proj_c, proj_c, proj_c, bias, q_norm.reshape(1, hd), k_norm.reshape(1, hd))
    return (res[0], res[1]) if ctx_out else (res[0], None)


def _pool_bands():
    t = SEQ_TILE
    i = np.arange(t)[:, None]
    j = np.arange(t)[None, :]
    bands = np.zeros((len(POOL_WINDOWS), 3, t, t), np.float32)
    for gi, win in enumerate(POOL_WINDOWS):
        half = win // 2
        for s, off in enumerate((-t, 0, t)):
            jj = j + off
            bands[gi, s] = (jj >= i - half) & (jj <= i + half - 1)
    return jnp.asarray(bands, BF16)


def _pool_kernel(seq_len, u_ref, up_ref, un_ref, band_ref, pw_ref, sc_ref, o_ref):
    t = pl.program_id(1)
    nt = pl.num_programs(1)
    tl = u_ref.shape[1]
    half = lax.shift_left(1, pl.program_id(2))
    has_prev = (t > 0).astype(F32)
    has_next = (t < nt - 1).astype(F32)
    pos = t * tl + lax.broadcasted_iota(jnp.int32, (tl, 1), 0)
    cur = u_ref[0]
    wsum = (_dot(band_ref[0, 1], cur) + has_prev * _dot(band_ref[0, 0], up_ref[0])
            + has_next * _dot(band_ref[0, 2], un_ref[0]))
    cnt = jnp.minimum(pos + half, seq_len) - jnp.maximum(pos - half, 0)
    d = wsum / cnt.astype(F32) - cur.astype(F32)
    o_ref[0] = (_dot(d.astype(BF16), pw_ref[0]) * sc_ref[...]).astype(o_ref.dtype)


def _pool(proj, bands, pool_w, pool_scale):
    assert POOL_WINDOWS == tuple(2 << g for g in range(len(POOL_WINDOWS)))
    bsz, l, _ = proj.shape
    tl = SEQ_TILE
    nt = l // tl
    ng = len(POOL_WINDOWS)
    cb = POOL_COL // LANES
    return pl.pallas_call(
        functools.partial(_pool_kernel, l),
        grid=(bsz, nt, ng),
        in_specs=[pl.BlockSpec((1, tl, LANES), lambda b, t, g: (b, t, cb + g)),
                  pl.BlockSpec((1, tl, LANES), lambda b, t, g: (b, jnp.maximum(t - 1, 0), cb + g)),
                  pl.BlockSpec((1, tl, LANES), lambda b, t, g: (b, jnp.minimum(t + 1, nt - 1), cb + g)),
                  pl.BlockSpec((1,) + bands.shape[1:], lambda b, t, g: (g, 0, 0, 0)),
                  pl.BlockSpec((1, LANES, LANES), lambda b, t, g: (g, 0, 0)),
                  pl.BlockSpec((1, LANES), lambda b, t, g: (0, g))],
        out_specs=pl.BlockSpec((1, tl, LANES), lambda b, t, g: (b, t, g)),
        out_shape=jax.ShapeDtypeStruct((bsz, l, ng * LANES), BF16),
        compiler_params=_cparams("parallel", "parallel", "parallel"),
        name="pool_mixer",
    )(proj, proj, proj, bands, pool_w, pool_scale.reshape(1, ng * LANES))


def _gdn_prep_kernel(x_ref, xp_ref, xn_ref, w_ref, o_ref):
    t = pl.program_id(1)
    j = pl.program_id(2)
    tl = x_ref.shape[1]
    halo = xp_ref.shape[1]
    x = x_ref[0].astype(F32)
    row = lax.broadcasted_iota(jnp.int32, x.shape, 0)
    prev_row = jnp.where(t > 0, xp_ref[0, halo - 1:halo, :].astype(F32), 0.0)
    next_row = jnp.where(t < pl.num_programs(1) - 1, xn_ref[0, 0:1, :].astype(F32), 0.0)
    x_m1 = jnp.where(row == 0, prev_row, pltpu.roll(x, 1, axis=0))
    x_p1 = jnp.where(row == tl - 1, next_row, pltpu.roll(x, tl - 1, axis=0))
    y = _silu(w_ref[0:1, :] * x_m1 + w_ref[1:2, :] * x + w_ref[2:3, :] * x_p1)
    third = pl.num_programs(2) // 3
    for hh in range(y.shape[1] // HEAD_DIM):
        cs = slice(hh * HEAD_DIM, (hh + 1) * HEAD_DIM)
        yh = y[:, cs]
        inv = lax.rsqrt(jnp.sum(yh * yh, axis=-1, keepdims=True) + EPS)
        fac = jnp.where(j < third, inv * HEAD_DIM ** -0.5, jnp.where(j < 2 * third, inv, 1.0))
        o_ref[0, :, cs] = (yh * fac).astype(o_ref.dtype)


def _gdn_prep(proj, conv_w):
    bsz, l, _ = proj.shape
    width = conv_w.shape[1]
    tl = min(l, 4 * SEQ_TILE)
    tc = 2 * HEAD_DIM
    halo = 16
    nt = l // tl
    cb = GDN_COL // tc
    nh = l // halo
    return pl.pallas_call(
        _gdn_prep_kernel,
        grid=(bsz, nt, width // tc),
        in_specs=[pl.BlockSpec((1, tl, tc), lambda b, t, j: (b, t, cb + j)),
                  pl.BlockSpec((1, halo, tc), lambda b, t, j: (b, jnp.maximum(t * (tl // halo) - 1, 0), cb + j)),
                  pl.BlockSpec((1, halo, tc), lambda b, t, j: (b, jnp.minimum((t + 1) * (tl // halo), nh - 1), cb + j)),
                  pl.BlockSpec((3, tc), lambda b, t, j: (0, j))],
        out_specs=pl.BlockSpec((1, tl, tc), lambda b, t, j: (b, t, j)),
        out_shape=jax.ShapeDtypeStruct((bsz, l, width), BF16),
        compiler_params=_cparams("parallel", "parallel", "parallel"),
        name="gdn_prep",
    )(proj, proj, proj, conv_w)


def _gdn_gates_kernel(ab_ref, neg_a_ref, dt_ref, is_a_ref, is_bwd_ref, o_ref):
    x = ab_ref[0]
    tl = x.shape[0]
    z = x + dt_ref[...]
    g = neg_a_ref[...] * (jnp.maximum(z, 0.0) + jnp.log(1.0 + jnp.exp(-jnp.abs(z))))
    beta = 1.0 / (1.0 + jnp.exp(-x))
    pos = lax.broadcasted_iota(jnp.int32, x.shape, 0) % GDN_CHUNK
    fwd = g
    bwd = g
    s = 1
    while s < GDN_CHUNK:
        fwd = fwd + jnp.where(pos >= s, pltpu.roll(fwd, s, axis=0), 0.0)
        bwd = bwd + jnp.where(pos < GDN_CHUNK - s, pltpu.roll(bwd, tl - s, axis=0), 0.0)
        s *= 2
    gc = jnp.where(is_bwd_ref[...] > 0, bwd, fwd)
    o_ref[0] = jnp.where(is_a_ref[...] > 0, gc, beta)


def _gdn_gates(ab, a_log, dt_bias):
    bsz, l, _ = ab.shape
    nh = GDN_HEADS
    neg_a = jnp.zeros((2, 2, nh), F32).at[:, 0].set(-jnp.exp(a_log.astype(F32)))
    dt = jnp.zeros((2, 2, nh), F32).at[:, 0].set(dt_bias.astype(F32))
    is_a = jnp.zeros((2, 2, nh), F32).at[:, 0].set(1.0)
    is_bwd = jnp.zeros((2, 2, nh), F32).at[1].set(1.0)
    lanes = lambda v: jnp.pad(v.reshape(1, 4 * nh), ((0, 0), (0, LANES - 4 * nh)))
    tl = SEQ_TILE
    vec = pl.BlockSpec((1, LANES), lambda b, t: (0, 0))
    return pl.pallas_call(
        _gdn_gates_kernel,
        grid=(bsz, l // tl),
        in_specs=[pl.BlockSpec((1, tl, LANES), lambda b, t: (b, t, 0)), vec, vec, vec, vec],
        out_specs=pl.BlockSpec((1, tl, LANES), lambda b, t: (b, t, 0)),
        out_shape=jax.ShapeDtypeStruct((bsz, l, LANES), F32),
        compiler_params=_cparams("parallel", "parallel"),
        name="gdn_gates",
    )(ab, lanes(neg_a), lanes(dt), lanes(is_a), lanes(is_bwd))


def _gdn_local_kernel(q_ref, k_ref, v_ref, gcol_ref, grow_ref, qo_f_ref, mn_f_ref, qo_b_ref, mn_b_ref):
    cs = GDN_CHUNK
    hd = HEAD_DIM
    tl = q_ref.shape[1]
    ii = lax.broadcasted_iota(jnp.int32, (tl, tl), 0)
    jj = lax.broadcasted_iota(jnp.int32, (tl, tl), 1)
    same = (ii // cs) == (jj // cs)
    outs = ((qo_f_ref, mn_f_ref), (qo_b_ref, mn_b_ref))
    q = q_ref[0].astype(F32)
    k = k_ref[0].astype(F32)
    v = v_ref[0].astype(F32)
    k16 = k_ref[0]
    qk = _dot_nt(q_ref[0], k16)
    kk = _dot_nt(k16, k16)
    for d in range(2):
        qo_ref, mn_ref = outs[d]
        gc = gcol_ref[0, 0, :, 2 * d:2 * d + 1]
        beta = gcol_ref[0, 0, :, 2 * d + 1:2 * d + 2]
        gr = grow_ref[0, 0, 2 * d:2 * d + 1, :]
        incl = jnp.logical_and(same, (ii >= jj) if d == 0 else (ii <= jj))
        strict = jnp.logical_and(same, (ii > jj) if d == 0 else (ii < jj))
        decay = jnp.where(incl, jnp.exp(jnp.where(incl, gc - gr, 0.0)), 0.0)
        pw = jnp.where(strict, kk * decay, 0.0) * (-beta)
        acc = pw
        s = 2
        while s < cs:
            p16 = pw.astype(BF16)
            pw = _dot(p16, p16)
            acc = acc + pw + _dot(acc.astype(BF16), pw.astype(BF16))
            s *= 2
        e_gc = jnp.exp(gc)
        rhs = jnp.concatenate([v * beta, k * (beta * e_gc)], axis=-1)
        uw16 = (rhs + _dot(acc.astype(BF16), rhs.astype(BF16))).astype(BF16)
        au_aw = _dot((qk * decay).astype(BF16), uw16)
        qo_ref[0, 0] = jnp.concatenate([q * e_gc - au_aw[:, hd:], au_aw[:, :hd]], axis=-1).astype(qo_ref.dtype)
        edge = jnp.logical_and(same, (jj % cs) == (cs - 1 if d == 0 else 0))
        g_last = jnp.sum(jnp.where(edge, gr, 0.0), axis=-1, keepdims=True)
        kd16 = (k * jnp.exp(g_last - gc)).astype(BF16)
        for c in range(tl // cs):
            rs = slice(c * cs, (c + 1) * cs)
            ku_kw = _dot_tn(kd16[rs], uw16[rs])
            mn_ref[0, 0, c * hd:(c + 1) * hd, :] = jnp.concatenate([-ku_kw[:, hd:], ku_kw[:, :hd]], axis=-1).astype(mn_ref.dtype)


def _gdn_local(qkv, gcol, grow):
    bsz, l, _ = qkv.shape
    nh = GDN_HEADS
    hd = HEAD_DIM
    tl = SEQ_TILE
    per_tile = tl // GDN_CHUNK * hd
    col = lambda off: pl.BlockSpec((1, tl, hd), lambda b, h, t: (b, t, off + h))
    qo = pl.BlockSpec((1, 1, tl, 2 * hd), lambda b, h, t: (b, h, t, 0))
    mn = pl.BlockSpec((1, 1, per_tile, 2 * hd), lambda b, h, t: (b, h, t, 0))
    s_qo = jax.ShapeDtypeStruct((bsz, nh, l, 2 * hd), BF16)
    s_mn = jax.ShapeDtypeStruct((bsz, nh, l // GDN_CHUNK * hd, 2 * hd), BF16)
    return pl.pallas_call(
        _gdn_local_kernel,
        grid=(bsz, nh, l // tl),
        in_specs=[col(0), col(nh), col(2 * nh),
                  pl.BlockSpec((1, 1, tl, 4), lambda b, h, t: (b, h, t, 0)),
                  pl.BlockSpec((1, 1, 4, tl), lambda b, h, t: (b, h, 0, t))],
        out_specs=[qo, mn, qo, mn],
        out_shape=[s_qo, s_mn, s_qo, s_mn],
        compiler_params=_cparams("parallel", "parallel", "parallel"),
        name="gdn_local",
    )(qkv, qkv, qkv, gcol, grow)


GDN_SEQ_REFS = 6


def _gdn_scan_kernel(ctx_out, *refs):
    lat = refs[0:GDN_SEQ_REFS]
    ctx = refs[GDN_SEQ_REFS:2 * GDN_SEQ_REFS]
    nw_ref = refs[2 * GDN_SEQ_REFS]
    if ctx_out:
        o_ref, oc_ref, of_ref, ob_ref, ocf_ref, ocb_ref = refs[2 * GDN_SEQ_REFS + 1:]
    else:
        o_ref, of_ref, ob_ref = refs[2 * GDN_SEQ_REFS + 1:]
        oc_ref = ocf_ref = ocb_ref = None
    cs = GDN_CHUNK
    hd = HEAD_DIM

    def scan(seq_refs, out_f, out_b, state):
        qo_f, mn_f, qo_b, mn_b, gcol, _ = seq_refs
        n = qo_f.shape[2] // cs

        def one(qo, mn, c, g_row, g_lane, s, out):
            s16 = s.astype(BF16)
            r0 = pl.multiple_of(c * cs, cs)
            m0 = pl.multiple_of(c * hd, hd)
            if out is not None:
                out[pl.ds(r0, cs), :] = (_dot(qo[0, 0, pl.ds(r0, cs), 0:hd], s16)
                                         + qo[0, 0, pl.ds(r0, cs), hd:2 * hd].astype(F32))
            e_last = jnp.exp(gcol[0, 0, pl.ds(g_row, 1), g_lane:g_lane + 1])
            return (s * e_last + _dot(mn[0, 0, pl.ds(m0, hd), 0:hd], s16)
                    + mn[0, 0, pl.ds(m0, hd), hd:2 * hd].astype(F32))

        def step(i, st):
            s_f, s_b = st
            cb = n - 1 - i
            s_f = one(qo_f, mn_f, i, i * cs + cs - 1, 0, s_f, out_f)
            s_b = one(qo_b, mn_b, cb, cb * cs, 2, s_b, out_b)
            return s_f, s_b

        return lax.fori_loop(0, n, step, state)

    def finish(seq_refs, out_f, out_b, dst):
        z = seq_refs[GDN_SEQ_REFS - 1][0].astype(F32)
        o = _rms(out_f[...] + out_b[...], nw_ref[...])
        dst[0] = (o * _silu(z)).astype(dst.dtype)

    zero = jnp.zeros((hd, hd), F32)
    state = scan(ctx, ocf_ref, ocb_ref, (zero, zero))
    scan(lat, of_ref, ob_ref, state)
    finish(lat, of_ref, ob_ref, o_ref)
    if ctx_out:
        finish(ctx, ocf_ref, ocb_ref, oc_ref)


def _gdn_scan(loc_l, gcol_l, proj_l, loc_c, gcol_c, proj_c, norm_w, ctx_out):
    bsz, nh, seq, _ = loc_l[0].shape
    n_ctx = loc_c[0].shape[2]
    hd = HEAD_DIM
    z_col = Z_COL // hd

    def specs(l):
        full = lambda r, w: pl.BlockSpec((1, 1, r, w), lambda b, h: (b, h, 0, 0))
        one_dir = [full(l, 2 * hd), full(l // GDN_CHUNK * hd, 2 * hd)]
        return one_dir * 2 + [full(l, 4), pl.BlockSpec((1, l, hd), lambda b, h: (b, 0, z_col + h))]

    out_shape = [jax.ShapeDtypeStruct((bsz, seq, nh * hd), BF16)]
    out_specs = [pl.BlockSpec((1, seq, hd), lambda b, h: (b, 0, h))]
    scratch = [pltpu.VMEM((seq, hd), F32), pltpu.VMEM((seq, hd), F32)]
    if ctx_out:
        out_shape.append(jax.ShapeDtypeStruct((bsz, n_ctx, nh * hd), BF16))
        out_specs.append(pl.BlockSpec((1, n_ctx, hd), lambda b, h: (b, 0, h)))
        scratch += [pltpu.VMEM((n_ctx, hd), F32), pltpu.VMEM((n_ctx, hd), F32)]
    res = pl.pallas_call(
        functools.partial(_gdn_scan_kernel, ctx_out),
        grid=(bsz, nh),
        in_specs=specs(seq) + specs(n_ctx) + [pl.BlockSpec((1, hd), lambda b, h: (0, 0))],
        out_specs=out_specs,
        out_shape=out_shape,
        scratch_shapes=scratch,
        compiler_params=_cparams("parallel", "parallel"),
        name="gdn_scan",
    )(*loc_l, gcol_l, proj_l, *loc_c, gcol_c, proj_c, norm_w.reshape(1, hd))
    return (res[0], res[1]) if ctx_out else (res[0], None)


def _gate_layouts(gb):
    bsz, l, _ = gb.shape
    nh = GDN_HEADS
    t = gb[:, :, :4 * nh].reshape(bsz, l, 2, 2, nh)
    gcol = jnp.transpose(t, (0, 4, 1, 2, 3)).reshape(bsz, nh, l, 4)
    grow = jnp.transpose(t, (0, 4, 2, 3, 1)).reshape(bsz, nh, 4, l)
    return gcol, grow


NA_WIDTH = NA_HEADS * HEAD_DIM
POOL_WIDTH = LANES * len(POOL_WINDOWS)
GDN_WIDTH = GDN_HEADS * HEAD_DIM
POOL_COL = 3 * NA_WIDTH
GDN_COL = POOL_COL + POOL_WIDTH
Z_COL = GDN_COL + 3 * GDN_WIDTH
AB_COL = Z_COL + GDN_WIDTH


def _token_mixers(proj_l, ab_l, proj_c, ab_c, lw, ctx_out):
    y_na_l, y_na_c = _na(proj_l, proj_c, lw["na_bias"], lw["na_q_norm"], lw["na_k_norm"], ctx_out)
    y_pool_l = _pool(proj_l, lw["bands"], lw["pool_w"], lw["pool_scale"])
    y_pool_c = _pool(proj_c, lw["bands"], lw["pool_w"], lw["pool_scale"]) if ctx_out else None

    def local(proj, ab):
        gb = _gdn_gates(ab, lw["gdn_a_log"], lw["gdn_dt_bias"])
        gcol, grow = _gate_layouts(gb)
        return _gdn_local(_gdn_prep(proj, lw["gdn_conv"]), gcol, grow), gcol

    loc_l, gcol_l = local(proj_l, ab_l)
    loc_c, gcol_c = local(proj_c, ab_c)
    y_gdn_l, y_gdn_c = _gdn_scan(loc_l, gcol_l, proj_l, loc_c, gcol_c, proj_c, lw["gdn_norm"], ctx_out)
    return (y_na_l, y_pool_l, y_gdn_l), (y_na_c, y_pool_c, y_gdn_c)


def kernel(x, c, ctx, c_ctx, w_mod, b_mod, w_in, w_out, na_q_norm, na_k_norm, na_rpb, pool_w, pool_scale,
           gdn_conv, gdn_a_log, gdn_dt_bias, gdn_norm, ffn_w_gate, ffn_w_up, ffn_w_down,
           moe_router, moe_w_gate, moe_w_up, moe_w_down):
    bsz, seq, d = x.shape
    n_ctx = ctx.shape[1]
    depth = w_mod.shape[0]
    assert seq % (GRID_W * NA_Q_ROWS) == 0 and seq // GRID_W >= NA_K_ROWS
    assert seq % SEQ_TILE == 0 and n_ctx % SEQ_TILE == 0 and w_in.shape[2] == AB_COL + 4 * GDN_HEADS

    mod_rows = 16
    c_rows = jnp.zeros((mod_rows, d), F32).at[:bsz].set(c).at[bsz].set(c_ctx)
    mods = _modulation(c_rows, w_mod, b_mod).reshape(depth, mod_rows, 6, d)
    bands = _pool_bands()

    xl = x
    xc = ctx.reshape(1, bsz * n_ctx, d)
    for i in range(depth):
        last = i == depth - 1
        mod_l = mods[i, :bsz]
        mod_c = mods[i, bsz:bsz + 1]
        w_main = w_in[i, :, :AB_COL].astype(BF16)
        w_ab = jnp.pad(w_in[i, :, AB_COL:], ((0, 0), (0, LANES - 4 * GDN_HEADS))).astype(BF16)
        wo = w_out[i].astype(BF16)
        ws = (wo[:NA_WIDTH], wo[NA_WIDTH:NA_WIDTH + POOL_WIDTH], wo[NA_WIDTH + POOL_WIDTH:])
        lw = dict(na_bias=_na_bias_table(na_rpb[i], seq // GRID_W), na_q_norm=na_q_norm[i], na_k_norm=na_k_norm[i],
                  bands=bands, pool_w=pool_w[i].astype(BF16), pool_scale=pool_scale[i], gdn_conv=gdn_conv[i],
                  gdn_a_log=gdn_a_log[i], gdn_dt_bias=gdn_dt_bias[i], gdn_norm=gdn_norm[i])

        h_l = _norm_mod(xl, mod_l, 0).reshape(bsz * seq, d)
        h_c = _norm_mod(xc, mod_c, 0).reshape(bsz * n_ctx, d)
        half = AB_COL // 2
        proj_l = _mm(h_l, w_main, BF16, TOK_TILE, half).reshape(bsz, seq, AB_COL)
        proj_c = _mm(h_c, w_main, BF16, TOK_TILE, half).reshape(bsz, n_ctx, AB_COL)
        ab_l = _mm(h_l, w_ab, F32, 2048, LANES).reshape(bsz, seq, LANES)
        ab_c = _mm(h_c, w_ab, F32, 2048, LANES).reshape(bsz, n_ctx, LANES)
        ys_l, ys_c = _token_mixers(proj_l, ab_l, proj_c, ab_c, lw, not last)
        xl = _out_proj(xl, mod_l, 2, ys_l, ws)
        if not last:
            xc = _out_proj(xc, mod_c, 2, [y.reshape(1, bsz * n_ctx, -1) for y in ys_c], ws)

        streams = [(xl, mod_l)] if last else [(xl, mod_l), (xc, mod_c)]
        outs = []
        j = i // 2
        for xs, mod in streams:
            if i % 2 == 0:
                h = _norm_mod(xs, mod, 3)
                outs.append(_ffn(xs, mod, 5, h, ffn_w_gate[j].astype(BF16), ffn_w_up[j].astype(BF16),
                                 ffn_w_down[j].astype(BF16)))
            else:
                h, h_packed = _norm_mod(xs, mod, 3, packed=True)
                rw = jnp.pad(moe_router[j], ((0, 0), (0, LANES - N_EXPERTS))).astype(BF16)
                outs.append(_moe(xs, mod, 5, h, h_packed, rw, moe_w_gate[j].astype(BF16), moe_w_up[j].astype(BF16),
                                 moe_w_down[j].astype(BF16)))
        xl = outs[0]
        if not last:
            xc = outs[1]
    return xl
```

```python
import functools

import numpy as np
import jax
import jax.numpy as jnp
from jax import lax
from jax.experimental import pallas as pl
from jax.experimental.pallas import tpu as pltpu

F32 = jnp.float32
BF16 = jnp.bfloat16

EPS = 1e-6
HEAD_DIM = 128
GRID_W = 64
NA_HEADS = 6
NA_WIN_R = 8
NA_WIN_C = 16
NA_Q_ROWS = 4
NA_K_ROWS = NA_Q_ROWS + NA_WIN_R
NA_GROUPS_PER_STEP = 2
POOL_WINDOWS = (2, 4, 8, 16)
GDN_HEADS = 6
GDN_CHUNK = 64
N_EXPERTS = 8
LANES = 128
MASK_VALUE = -1e30
VMEM_LIMIT = 56 * 1024 * 1024
SEQ_TILE = 256
TOK_TILE = 512
MOE_TILE = 512
GATHER_TILE = 256
GDN_LOCAL_ROWS = 1024


def _cparams(*sem):
    return pltpu.CompilerParams(dimension_semantics=sem, vmem_limit_bytes=VMEM_LIMIT)


def _silu(x):
    return x / (1.0 + jnp.exp(-x))


def _dot(a, b):
    return jnp.dot(a, b, preferred_element_type=F32)


def _dot_nt(a, b):
    return lax.dot_general(a, b, (((1,), (1,)), ((), ())), preferred_element_type=F32)


def _dot_tn(a, b):
    return lax.dot_general(a, b, (((0,), (0,)), ((), ())), preferred_element_type=F32)


def _rms(x, w):
    xf = x.astype(F32)
    return xf * lax.rsqrt(jnp.mean(xf * xf, axis=-1, keepdims=True) + EPS) * w


def _mod_kernel(a_ref, w_ref, b_ref, o_ref):
    a = _silu(a_ref[...])
    o_ref[0] = _dot(a.astype(BF16), w_ref[0].astype(BF16)) + b_ref[0]


def _modulation(c_rows, w_mod, b_mod):
    depth, d, n = w_mod.shape
    r = c_rows.shape[0]
    tn = 1024
    return pl.pallas_call(
        _mod_kernel,
        grid=(depth, n // tn),
        in_specs=[pl.BlockSpec((r, d), lambda i, j: (0, 0)),
                  pl.BlockSpec((1, d, tn), lambda i, j: (i, 0, j)),
                  pl.BlockSpec((1, 1, tn), lambda i, j: (i, 0, j))],
        out_specs=pl.BlockSpec((1, r, tn), lambda i, j: (i, 0, j)),
        out_shape=jax.ShapeDtypeStruct((depth, r, n), F32),
        compiler_params=_cparams("parallel", "parallel"),
        name="modulation",
    )(c_rows, w_mod, b_mod.reshape(depth, 1, n))


HIGH_HALF = 0xFFFF0000


def _pack_bf16_pairs(h):
    half = h.shape[1] // 2
    hf = h.astype(F32)
    lo = lax.bitcast_convert_type(hf[:, :half], jnp.uint32) >> 16
    hi = lax.bitcast_convert_type(hf[:, half:], jnp.uint32) & jnp.uint32(HIGH_HALF)
    return hi | lo


def _unpack_bf16_pairs(w):
    lo = lax.bitcast_convert_type(w << 16, F32).astype(BF16)
    hi = lax.bitcast_convert_type(w & jnp.uint32(HIGH_HALF), F32).astype(BF16)
    return lo, hi


def _norm_mod_kernel(si, x_ref, m_ref, o_ref, *packed_ref):
    x = x_ref[0]
    y = x * lax.rsqrt(jnp.mean(x * x, axis=-1, keepdims=True) + EPS)
    h = (y * (1.0 + m_ref[0, si + 1:si + 2, :]) + m_ref[0, si:si + 1, :]).astype(BF16)
    o_ref[0] = h
    if packed_ref:
        packed_ref[0][0] = _pack_bf16_pairs(h)


def _norm_mod(x, mod, si, packed=False):
    g, l, d = x.shape
    tl = min(l, TOK_TILE)
    out_shape = [jax.ShapeDtypeStruct((g, l, d), BF16)]
    out_specs = [pl.BlockSpec((1, tl, d), lambda b, t: (b, t, 0))]
    if packed:
        out_shape.append(jax.ShapeDtypeStruct((g, l, d // 2), jnp.uint32))
        out_specs.append(pl.BlockSpec((1, tl, d // 2), lambda b, t: (b, t, 0)))
    res = pl.pallas_call(
        functools.partial(_norm_mod_kernel, si),
        grid=(g, l // tl),
        in_specs=[pl.BlockSpec((1, tl, d), lambda b, t: (b, t, 0)),
                  pl.BlockSpec((1, 6, d), lambda b, t: (b, 0, 0))],
        out_specs=out_specs,
        out_shape=out_shape,
        compiler_params=_cparams("parallel", "parallel"),
        name="norm_mod",
    )(x, mod)
    return tuple(res) if packed else res[0]


def _mm_kernel(a_ref, w_ref, o_ref):
    o_ref[...] = _dot(a_ref[...], w_ref[...]).astype(o_ref.dtype)


def _mm(a, w, out_dtype, tm, tn):
    m, k = a.shape
    n = w.shape[1]
    tm = min(tm, m)
    return pl.pallas_call(
        _mm_kernel,
        grid=(n // tn, m // tm),
        in_specs=[pl.BlockSpec((tm, k), lambda j, i: (i, 0)),
                  pl.BlockSpec((k, tn), lambda j, i: (0, j))],
        out_specs=pl.BlockSpec((tm, tn), lambda j, i: (i, j)),
        out_shape=jax.ShapeDtypeStruct((m, n), out_dtype),
        compiler_params=_cparams("parallel", "parallel"),
        name="matmul",
    )(a, w)


def _out_proj_kernel(gi, x_ref, m_ref, y0_ref, y1_ref, y2_ref, w0_ref, w1_ref, w2_ref, o_ref):
    acc = _dot(y0_ref[0], w0_ref[...]) + _dot(y1_ref[0], w1_ref[...]) + _dot(y2_ref[0], w2_ref[...])
    o_ref[0] = x_ref[0] + m_ref[0, gi:gi + 1, :] * acc


def _out_proj(x, mod, gi, ys, ws):
    g, l, d = x.shape
    tl = min(l, TOK_TILE)
    y_specs = [pl.BlockSpec((1, tl, y.shape[2]), lambda b, t: (b, t, 0)) for y in ys]
    w_specs = [pl.BlockSpec(w.shape, lambda b, t: (0, 0)) for w in ws]
    return pl.pallas_call(
        functools.partial(_out_proj_kernel, gi),
        grid=(g, l // tl),
        in_specs=[pl.BlockSpec((1, tl, d), lambda b, t: (b, t, 0)),
                  pl.BlockSpec((1, 6, d), lambda b, t: (b, 0, 0))] + y_specs + w_specs,
        out_specs=pl.BlockSpec((1, tl, d), lambda b, t: (b, t, 0)),
        out_shape=jax.ShapeDtypeStruct((g, l, d), F32),
        compiler_params=_cparams("parallel", "parallel"),
        name="out_proj",
    )(x, mod, *ys, *ws)


def _ffn_kernel(gi, x_ref, m_ref, h_ref, wg_ref, wu_ref, wd_ref, o_ref, acc_ref):
    f = pl.program_id(2)
    h = h_ref[0]
    a = _silu(_dot(h, wg_ref[...])) * _dot(h, wu_ref[...])
    part = _dot(a.astype(BF16), wd_ref[...])

    @pl.when(f == 0)
    def _():
        acc_ref[...] = part

    @pl.when(f > 0)
    def _():
        acc_ref[...] += part

    @pl.when(f == pl.num_programs(2) - 1)
    def _():
        o_ref[0] = x_ref[0] + m_ref[0, gi:gi + 1, :] * acc_ref[...]


def _ffn(x, mod, gi, h, wg, wu, wd):
    g, l, d = x.shape
    dff = wg.shape[1]
    tl = min(l, TOK_TILE)
    tf = 512
    return pl.pallas_call(
        functools.partial(_ffn_kernel, gi),
        grid=(g, l // tl, dff // tf),
        in_specs=[pl.BlockSpec((1, tl, d), lambda b, t, f: (b, t, 0)),
                  pl.BlockSpec((1, 6, d), lambda b, t, f: (b, 0, 0)),
                  pl.BlockSpec((1, tl, d), lambda b, t, f: (b, t, 0)),
                  pl.BlockSpec((d, tf), lambda b, t, f: (0, f)),
                  pl.BlockSpec((d, tf), lambda b, t, f: (0, f)),
                  pl.BlockSpec((tf, d), lambda b, t, f: (f, 0))],
        out_specs=pl.BlockSpec((1, tl, d), lambda b, t, f: (b, t, 0)),
        out_shape=jax.ShapeDtypeStruct((g, l, d), F32),
        scratch_shapes=[pltpu.VMEM((tl, d), F32)],
        compiler_params=_cparams("parallel", "parallel", "arbitrary"),
        name="ffn",
    )(x, mod, h, wg, wu, wd)


def _router_kernel(h_ref, w_ref, o_ref):
    logits = _dot(h_ref[0], w_ref[...])
    lane = lax.broadcasted_iota(jnp.int32, logits.shape, 1).astype(F32)
    l1 = jnp.where(lane < N_EXPERTS, logits, -jnp.inf)
    m1 = jnp.max(l1, axis=-1, keepdims=True)
    i1 = jnp.min(jnp.where(l1 == m1, lane, float(LANES)), axis=-1, keepdims=True)
    l2 = jnp.where(lane == i1, -jnp.inf, l1)
    m2 = jnp.max(l2, axis=-1, keepdims=True)
    i2 = jnp.min(jnp.where(l2 == m2, lane, float(LANES)), axis=-1, keepdims=True)
    e2 = jnp.exp(m2 - m1)
    w1 = 1.0 / (1.0 + e2)
    w2 = e2 / (1.0 + e2)
    o_ref[0] = jnp.where(lane == 0, i1, jnp.where(lane == 1, i2, jnp.where(lane == 2, w1, jnp.where(lane == 3, w2, 0.0))))


def _router(h, w_pad):
    g, l, d = h.shape
    tl = min(l, TOK_TILE)
    return pl.pallas_call(
        _router_kernel,
        grid=(g, l // tl),
        in_specs=[pl.BlockSpec((1, tl, d), lambda b, t: (b, t, 0)),
                  pl.BlockSpec((d, LANES), lambda b, t: (0, 0))],
        out_specs=pl.BlockSpec((1, tl, LANES), lambda b, t: (b, t, 0)),
        out_shape=jax.ShapeDtypeStruct((g, l, LANES), F32),
        compiler_params=_cparams("parallel", "parallel"),
        name="router",
    )(h, w_pad)


def _dispatch_plan(info):
    t = info.shape[0] * info.shape[1]
    flat = info.reshape(t, LANES)
    e = jnp.concatenate([flat[:, 0], flat[:, 1]]).astype(jnp.int32)
    onehot = (e[:, None] == jnp.arange(N_EXPERTS, dtype=jnp.int32)[None, :]).astype(jnp.int32)
    csum = jnp.cumsum(onehot, axis=0)
    rank = jnp.sum(csum * onehot, axis=1) - 1
    padded = (csum[-1] + MOE_TILE - 1) // MOE_TILE * MOE_TILE
    ends = jnp.cumsum(padded)
    pos = jnp.sum(onehot * (ends - padded)[None, :], axis=1) + rank
    n_rows = 2 * t + N_EXPERTS * MOE_TILE
    tok = jnp.tile(jnp.arange(t, dtype=jnp.int32), 2)
    row_token = jnp.zeros((n_rows,), jnp.int32).at[pos].set(tok)
    tile_start = jnp.arange(n_rows // MOE_TILE, dtype=jnp.int32) * MOE_TILE
    tile_expert = jnp.minimum(jnp.sum((ends[None, :] <= tile_start[:, None]).astype(jnp.int32), axis=1), N_EXPERTS - 1)
    n_used = (ends[-1] // MOE_TILE).astype(jnp.int32).reshape(1)
    return row_token, tile_expert, n_used, pos[:t], pos[t:]


def _row_copy(src_ref, dst_ref, sem, src_row, dst_row):
    return pltpu.make_async_copy(src_ref.at[pl.ds(src_row, 1)], dst_ref.at[pl.ds(dst_row, 1)], sem)


def _gather_kernel(idx_ref, src_ref, o_ref, sem):
    n = o_ref.shape[0]

    def issue(r, carry):
        _row_copy(src_ref, o_ref, sem, idx_ref[0, 0, r], r).start()
        return carry

    def wait(r, carry):
        _row_copy(src_ref, o_ref, sem, 0, 0).wait()
        return carry

    lax.fori_loop(0, n, issue, 0, unroll=8)
    lax.fori_loop(0, n, wait, 0, unroll=8)


def _gather_rows(src, idx):
    n = idx.shape[0]
    w = src.shape[1]
    tg = MOE_TILE
    return pl.pallas_call(
        _gather_kernel,
        grid=(n // tg,),
        in_specs=[pl.BlockSpec((1, 1, tg), lambda i: (i, 0, 0), memory_space=pltpu.SMEM),
                  pl.BlockSpec(memory_space=pl.ANY)],
        out_specs=pl.BlockSpec((tg, w), lambda i: (i, 0)),
        out_shape=jax.ShapeDtypeStruct((n, w), src.dtype),
        scratch_shapes=[pltpu.SemaphoreType.DMA(())],
        compiler_params=_cparams("arbitrary"),
        name="moe_gather",
    )(idx.reshape(n // tg, 1, tg), src)


def _experts_kernel(te_ref, nu_ref, hp_ref, wg_ref, wu_ref, wd_ref, o_ref, acc_ref):
    i = pl.program_id(0)
    f = pl.program_id(1)
    nf = pl.num_programs(1)
    used = i < nu_ref[0]

    @pl.when(used)
    def _():
        lo, hi = _unpack_bf16_pairs(hp_ref[...])
        half = lo.shape[1]
        g = _dot(lo, wg_ref[0, :half, :]) + _dot(hi, wg_ref[0, half:, :])
        u = _dot(lo, wu_ref[0, :half, :]) + _dot(hi, wu_ref[0, half:, :])
        part = _dot((_silu(g) * u).astype(BF16), wd_ref[0])

        @pl.when(f == 0)
        def _():
            acc_ref[...] = part

        @pl.when(f > 0)
        def _():
            acc_ref[...] += part

        @pl.when(f == nf - 1)
        def _():
            o_ref[...] = acc_ref[...]

    @pl.when(jnp.logical_and(jnp.logical_not(used), f == nf - 1))
    def _():
        o_ref[...] = jnp.zeros_like(o_ref)


def _experts(h_sorted, tile_expert, n_used, wg, wu, wd):
    n, half = h_sorted.shape
    d = 2 * half
    dff = wg.shape[2]
    tm = MOE_TILE
    tf = 512
    grid_spec = pltpu.PrefetchScalarGridSpec(
        num_scalar_prefetch=2,
        grid=(n // tm, dff // tf),
        in_specs=[pl.BlockSpec((tm, half), lambda i, f, te, nu: (i, 0)),
                  pl.BlockSpec((1, d, tf), lambda i, f, te, nu: (te[i], 0, f)),
                  pl.BlockSpec((1, d, tf), lambda i, f, te, nu: (te[i], 0, f)),
                  pl.BlockSpec((1, tf, d), lambda i, f, te, nu: (te[i], f, 0))],
        out_specs=pl.BlockSpec((tm, d), lambda i, f, te, nu: (i, 0)),
        scratch_shapes=[pltpu.VMEM((tm, d), F32)],
    )
    return pl.pallas_call(
        _experts_kernel,
        grid_spec=grid_spec,
        out_shape=jax.ShapeDtypeStruct((n, d), F32),
        compiler_params=_cparams("arbitrary", "arbitrary"),
        name="moe_experts",
    )(tile_expert, n_used, h_sorted, wg, wu, wd)


def _combine_kernel(gi, p1_ref, p2_ref, x_ref, m_ref, info_ref, y_ref, o_ref, a_ref, b_ref, sem):
    n = x_ref.shape[1]

    def issue(r, carry):
        _row_copy(y_ref, a_ref, sem.at[0], p1_ref[0, 0, r], r).start()
        _row_copy(y_ref, b_ref, sem.at[1], p2_ref[0, 0, r], r).start()
        return carry

    def wait(r, carry):
        _row_copy(y_ref, a_ref, sem.at[0], 0, 0).wait()
        _row_copy(y_ref, b_ref, sem.at[1], 0, 0).wait()
        return carry

    lax.fori_loop(0, n, issue, 0, unroll=8)
    lax.fori_loop(0, n, wait, 0, unroll=8)
    info = info_ref[0]
    mix = info[:, 2:3] * a_ref[...] + info[:, 3:4] * b_ref[...]
    o_ref[0] = x_ref[0] + m_ref[0, gi:gi + 1, :] * mix


def _combine(x, mod, gi, info, y_sorted, pos1, pos2):
    g, l, d = x.shape
    tc = GATHER_TILE
    nt = l // tc
    idx_spec = pl.BlockSpec((1, 1, tc), lambda b, t: (b * nt + t, 0, 0), memory_space=pltpu.SMEM)
    return pl.pallas_call(
        functools.partial(_combine_kernel, gi),
        grid=(g, nt),
        in_specs=[idx_spec, idx_spec,
                  pl.BlockSpec((1, tc, d), lambda b, t: (b, t, 0)),
                  pl.BlockSpec((1, 6, d), lambda b, t: (b, 0, 0)),
                  pl.BlockSpec((1, tc, LANES), lambda b, t: (b, t, 0)),
                  pl.BlockSpec(memory_space=pl.ANY)],
        out_specs=pl.BlockSpec((1, tc, d), lambda b, t: (b, t, 0)),
        out_shape=jax.ShapeDtypeStruct((g, l, d), F32),
        scratch_shapes=[pltpu.VMEM((tc, d), F32), pltpu.VMEM((tc, d), F32), pltpu.SemaphoreType.DMA((2,))],
        compiler_params=_cparams("arbitrary", "arbitrary"),
        name="moe_combine",
    )(pos1.reshape(g * nt, 1, tc), pos2.reshape(g * nt, 1, tc), x, mod, info, y_sorted)


def _moe(x, mod, gi, h, h_packed, router_w, wg, wu, wd):
    g, l, d = x.shape
    info = _router(h, router_w)
    row_token, tile_expert, n_used, pos1, pos2 = _dispatch_plan(info)
    h_sorted = _gather_rows(h_packed.reshape(g * l, d // 2), row_token)
    y_sorted = _experts(h_sorted, tile_expert, n_used, wg, wu, wd)
    return _combine(x, mod, gi, info, y_sorted, pos1, pos2)


def _na_bias_table(rpb, rows):
    w = GRID_W
    col = np.arange(w)
    c0 = np.clip(col - NA_WIN_C // 2, 0, w - NA_WIN_C)
    col_in = (col[None, :] >= c0[:, None]) & (col[None, :] < c0[:, None] + NA_WIN_C)
    col_idx = np.clip(col[None, :] - col[:, None], 1 - NA_WIN_C, NA_WIN_C - 1) + NA_WIN_C - 1
    n_groups = rows // NA_Q_ROWS
    onehot = (col_idx[None] == np.arange(2 * NA_WIN_C - 1)[:, None, None]).astype(np.float32)
    blocks = jnp.einsum('hdc,cqk->hdqk', rpb.astype(F32), jnp.asarray(onehot), precision=lax.Precision.HIGHEST)
    blocks = jnp.where(col_in[None, None], blocks, MASK_VALUE)
    masked = jnp.full_like(blocks[:, 0], MASK_VALUE)
    variants = []
    for grp in (0, 1, n_groups - 1):
        r = grp * NA_Q_ROWS
        start = _na_key_start(r, rows)
        q_rows = []
        for a in range(NA_Q_ROWS):
            ws = int(np.clip(r + a - NA_WIN_R // 2, 0, rows - NA_WIN_R))
            row = []
            for i in range(NA_K_ROWS):
                kr = start + i
                row.append(blocks[:, kr - (r + a) + NA_WIN_R - 1] if ws <= kr < ws + NA_WIN_R else masked)
            q_rows.append(jnp.concatenate(row, axis=-1))
        variants.append(jnp.concatenate(q_rows, axis=-2))
    return jnp.stack(variants, axis=1)


def _na_key_start(r, rows):
    return int(np.clip(r - NA_WIN_R // 2, 0, rows - NA_K_ROWS))


def _na_kernel(ctx_out, q_ref, k_ref, v_ref, qc_ref, kc_ref, vc_ref, bias_ref, qw_ref, kw_ref, *rest):
    if ctx_out:
        o_ref, oc_ref, kn_ref = rest
    else:
        o_ref, kn_ref = rest
    n_ctx = kc_ref.shape[1]
    seq = k_ref.shape[1]
    rows = seq // GRID_W
    n_groups = rows // NA_Q_ROWS
    nq = NA_Q_ROWS * GRID_W
    nk = NA_K_ROWS * GRID_W
    scale = HEAD_DIM ** -0.5
    qw = qw_ref[...]
    kw = kw_ref[...]
    kn_ref[0:n_ctx, :] = _rms(kc_ref[0], kw).astype(BF16)

    def norm_keys(j, carry):
        r0 = pl.multiple_of(j * nq, nq)
        kn_ref[pl.ds(n_ctx + r0, nq), :] = _rms(k_ref[0, pl.ds(r0, nq), :], kw).astype(BF16)
        return carry

    lax.fori_loop(0, seq // nq, norm_keys, 0)
    k_ctx = kn_ref[0:n_ctx, :]
    v_ctx = vc_ref[0]

    def groups(it, carry):
        gs = [it * NA_GROUPS_PER_STEP + u for u in range(NA_GROUPS_PER_STEP)]
        q0 = [pl.multiple_of(g * nq, nq) for g in gs]
        k0 = [pl.multiple_of(jnp.clip(g * NA_Q_ROWS - NA_WIN_R // 2, 0, rows - NA_K_ROWS) * GRID_W, GRID_W) for g in gs]
        variant = [jnp.where(g == 0, 0, jnp.where(g == n_groups - 1, 2, 1)) for g in gs]
        qn = [(_rms(q_ref[0, pl.ds(q, nq), :], qw) * scale).astype(BF16) for q in q0]
        s = [_dot_nt(a, kn_ref[pl.ds(n_ctx + k, nk), :]) + bias_ref[0, var] for a, k, var in zip(qn, k0, variant)]
        sc = [_dot_nt(a, k_ctx) for a in qn]
        m = [jnp.maximum(jnp.max(a, axis=-1, keepdims=True), jnp.max(b, axis=-1, keepdims=True)) for a, b in zip(s, sc)]
        p = [jnp.exp(a - mm) for a, mm in zip(s, m)]
        pc = [jnp.exp(a - mm) for a, mm in zip(sc, m)]
        denom = [jnp.sum(a, axis=-1, keepdims=True) + jnp.sum(b, axis=-1, keepdims=True) for a, b in zip(p, pc)]
        o = [_dot(a.astype(BF16), v_ref[0, pl.ds(k, nk), :]) + _dot(b.astype(BF16), v_ctx) for a, b, k in zip(p, pc, k0)]
        for q, oo, dd in zip(q0, o, denom):
            o_ref[0, pl.ds(q, nq), :] = (oo / dd).astype(o_ref.dtype)
        return carry

    lax.fori_loop(0, n_groups // NA_GROUPS_PER_STEP, groups, 0)

    if ctx_out:
        qn = (_rms(qc_ref[0], qw) * scale).astype(BF16)
        sc = _dot_nt(qn, k_ctx)
        pc = jnp.exp(sc - jnp.max(sc, axis=-1, keepdims=True))
        o = _dot(pc.astype(BF16), v_ctx) / jnp.sum(pc, axis=-1, keepdims=True)
        oc_ref[0] = o.astype(oc_ref.dtype)


def _na(proj_l, proj_c, bias, q_norm, k_norm, ctx_out):
    bsz, seq, _ = proj_l.shape
    n_ctx = proj_c.shape[1]
    h_ = NA_HEADS
    hd = HEAD_DIM
    c0 = NA_COL // hd
    lat = lambda off: pl.BlockSpec((1, seq, hd), lambda h, b: (b, 0, c0 + off + h))
    ctx = lambda off: pl.BlockSpec((1, n_ctx, hd), lambda h, b: (b, 0, c0 + off + h))
    out_shape = [jax.ShapeDtypeStruct((bsz, seq, h_ * hd), BF16)]
    out_specs = [pl.BlockSpec((1, seq, hd), lambda h, b: (b, 0, h))]
    if ctx_out:
        out_shape.append(jax.ShapeDtypeStruct((bsz, n_ctx, h_ * hd), BF16))
        out_specs.append(pl.BlockSpec((1, n_ctx, hd), lambda h, b: (b, 0, h)))
    res = pl.pallas_call(
        functools.partial(_na_kernel, ctx_out),
        grid=(h_, bsz),
        in_specs=[lat(0), lat(h_), lat(2 * h_), ctx(0), ctx(h_), ctx(2 * h_),
                  pl.BlockSpec((1,) + bias.shape[1:], lambda h, b: (h, 0, 0, 0)),
                  pl.BlockSpec((1, hd), lambda h, b: (0, 0)),
                  pl.BlockSpec((1, hd), lambda h, b: (0, 0))],
        out_specs=out_specs,
        out_shape=out_shape,
        scratch_shapes=[pltpu.VMEM((n_ctx + seq, hd), BF16)],
        compiler_params=_cparams("parallel", "parallel"),
        name="neighbourhood_attention",
    )(proj_l, proj_l, proj_l, proj_c, proj_c, proj_c, bias, q_norm.reshape(1, hd), k_norm.reshape(1, hd))
    return (res[0], res[1]) if ctx_out else (res[0], None)


def _pool_bands():
    t = SEQ_TILE
    i = np.arange(t)[:, None]
    j = np.arange(t)[None, :]
    bands = np.zeros((len(POOL_WINDOWS), 3, t, t), np.float32)
    for gi, win in enumerate(POOL_WINDOWS):
        half = win // 2
        for s, off in enumerate((-t, 0, t)):
            jj = j + off
            bands[gi, s] = (jj >= i - half) & (jj <= i + half - 1)
    return jnp.asarray(bands, BF16)


def _pool_kernel(seq_len, u_ref, up_ref, un_ref, band_ref, pw_ref, sc_ref, o_ref):
    t = pl.program_id(1)
    nt = pl.num_programs(1)
    tl = u_ref.shape[1]
    has_prev = (t > 0).astype(F32)
    has_next = (t < nt - 1).astype(F32)
    pos = t * tl + lax.broadcasted_iota(jnp.int32, (tl, 1), 0)
    for gi, win in enumerate(POOL_WINDOWS):
        cs = slice(gi * LANES, (gi + 1) * LANES)
        cur = u_ref[0, :, cs]
        wsum = (_dot(band_ref[gi, 1], cur) + has_prev * _dot(band_ref[gi, 0], up_ref[0, :, cs])
                + has_next * _dot(band_ref[gi, 2], un_ref[0, :, cs]))
        cnt = jnp.minimum(pos + win // 2, seq_len) - jnp.maximum(pos - win // 2, 0)
        d = wsum / cnt.astype(F32) - cur.astype(F32)
        o_ref[0, :, cs] = (_dot(d.astype(BF16), pw_ref[gi]) * sc_ref[:, cs]).astype(o_ref.dtype)


def _pool(proj, bands, pool_w, pool_scale):
    assert POOL_COL == 0
    bsz, l, _ = proj.shape
    tl = SEQ_TILE
    nt = l // tl
    pw = POOL_WIDTH
    return pl.pallas_call(
        functools.partial(_pool_kernel, l),
        grid=(bsz, nt),
        in_specs=[pl.BlockSpec((1, tl, pw), lambda b, t: (b, t, 0)),
                  pl.BlockSpec((1, tl, pw), lambda b, t: (b, jnp.maximum(t - 1, 0), 0)),
                  pl.BlockSpec((1, tl, pw), lambda b, t: (b, jnp.minimum(t + 1, nt - 1), 0)),
                  pl.BlockSpec(bands.shape, lambda b, t: (0, 0, 0, 0)),
                  pl.BlockSpec(pool_w.shape, lambda b, t: (0, 0, 0)),
                  pl.BlockSpec((1, pw), lambda b, t: (0, 0))],
        out_specs=pl.BlockSpec((1, tl, pw), lambda b, t: (b, t, 0)),
        out_shape=jax.ShapeDtypeStruct((bsz, l, pw), BF16),
        compiler_params=_cparams("parallel", "parallel"),
        name="pool_mixer",
    )(proj, proj, proj, bands, pool_w, pool_scale.reshape(1, pw))


def _gdn_prep_kernel(x_ref, xp_ref, xn_ref, w_ref, o_ref):
    t = pl.program_id(1)
    j = pl.program_id(2)
    tl = x_ref.shape[1]
    halo = xp_ref.shape[1]
    x = x_ref[0].astype(F32)
    row = lax.broadcasted_iota(jnp.int32, x.shape, 0)
    prev_row = jnp.where(t > 0, xp_ref[0, halo - 1:halo, :].astype(F32), 0.0)
    next_row = jnp.where(t < pl.num_programs(1) - 1, xn_ref[0, 0:1, :].astype(F32), 0.0)
    x_m1 = jnp.where(row == 0, prev_row, pltpu.roll(x, 1, axis=0))
    x_p1 = jnp.where(row == tl - 1, next_row, pltpu.roll(x, tl - 1, axis=0))
    y = _silu(w_ref[0:1, :] * x_m1 + w_ref[1:2, :] * x + w_ref[2:3, :] * x_p1)
    third = pl.num_programs(2) // 3
    for hh in range(y.shape[1] // HEAD_DIM):
        cs = slice(hh * HEAD_DIM, (hh + 1) * HEAD_DIM)
        yh = y[:, cs]
        inv = lax.rsqrt(jnp.sum(yh * yh, axis=-1, keepdims=True) + EPS)
        fac = jnp.where(j < third, inv * HEAD_DIM ** -0.5, jnp.where(j < 2 * third, inv, 1.0))
        o_ref[0, :, cs] = (yh * fac).astype(o_ref.dtype)


def _gdn_prep(proj, conv_w):
    bsz, l, _ = proj.shape
    width = conv_w.shape[1]
    tl = min(l, 4 * SEQ_TILE)
    tc = 2 * HEAD_DIM
    halo = 16
    nt = l // tl
    cb = GDN_COL // tc
    nh = l // halo
    return pl.pallas_call(
        _gdn_prep_kernel,
        grid=(bsz, nt, width // tc),
        in_specs=[pl.BlockSpec((1, tl, tc), lambda b, t, j: (b, t, cb + j)),
                  pl.BlockSpec((1, halo, tc), lambda b, t, j: (b, jnp.maximum(t * (tl // halo) - 1, 0), cb + j)),
                  pl.BlockSpec((1, halo, tc), lambda b, t, j: (b, jnp.minimum((t + 1) * (tl // halo), nh - 1), cb + j)),
                  pl.BlockSpec((3, tc), lambda b, t, j: (0, j))],
        out_specs=pl.BlockSpec((1, tl, tc), lambda b, t, j: (b, t, j)),
        out_shape=jax.ShapeDtypeStruct((bsz, l, width), BF16),
        compiler_params=_cparams("parallel", "parallel", "parallel"),
        name="gdn_prep",
    )(proj, proj, proj, conv_w)


def _gdn_gates_kernel(ab_ref, neg_a_ref, dt_ref, is_a_ref, is_bwd_ref, o_ref):
    x = ab_ref[0]
    tl = x.shape[0]
    z = x + dt_ref[...]
    g = neg_a_ref[...] * (jnp.maximum(z, 0.0) + jnp.log(1.0 + jnp.exp(-jnp.abs(z))))
    beta = 1.0 / (1.0 + jnp.exp(-x))
    pos = lax.broadcasted_iota(jnp.int32, x.shape, 0) % GDN_CHUNK
    fwd = g
    bwd = g
    s = 1
    while s < GDN_CHUNK:
        fwd = fwd + jnp.where(pos >= s, pltpu.roll(fwd, s, axis=0), 0.0)
        bwd = bwd + jnp.where(pos < GDN_CHUNK - s, pltpu.roll(bwd, tl - s, axis=0), 0.0)
        s *= 2
    gc = jnp.where(is_bwd_ref[...] > 0, bwd, fwd)
    o_ref[0] = jnp.where(is_a_ref[...] > 0, gc, beta)


def _gdn_gates(ab, a_log, dt_bias):
    bsz, l, _ = ab.shape
    nh = GDN_HEADS
    neg_a = jnp.zeros((2, 2, nh), F32).at[:, 0].set(-jnp.exp(a_log.astype(F32)))
    dt = jnp.zeros((2, 2, nh), F32).at[:, 0].set(dt_bias.astype(F32))
    is_a = jnp.zeros((2, 2, nh), F32).at[:, 0].set(1.0)
    is_bwd = jnp.zeros((2, 2, nh), F32).at[1].set(1.0)
    lanes = lambda v: jnp.pad(v.reshape(1, 4 * nh), ((0, 0), (0, LANES - 4 * nh)))
    tl = SEQ_TILE
    vec = pl.BlockSpec((1, LANES), lambda b, t: (0, 0))
    return pl.pallas_call(
        _gdn_gates_kernel,
        grid=(bsz, l // tl),
        in_specs=[pl.BlockSpec((1, tl, LANES), lambda b, t: (b, t, 0)), vec, vec, vec, vec],
        out_specs=pl.BlockSpec((1, tl, LANES), lambda b, t: (b, t, 0)),
        out_shape=jax.ShapeDtypeStruct((bsz, l, LANES), F32),
        compiler_params=_cparams("parallel", "parallel"),
        name="gdn_gates",
    )(ab, lanes(neg_a), lanes(dt), lanes(is_a), lanes(is_bwd))


def _gdn_local_kernel(q_ref, k_ref, v_ref, gcol_ref, grow_ref, qo_f_ref, mn_f_ref, qo_b_ref, mn_b_ref):
    cs = GDN_CHUNK
    hd = HEAD_DIM
    tl = SEQ_TILE
    n_sub = q_ref.shape[1] // tl
    ii = lax.broadcasted_iota(jnp.int32, (tl, tl), 0)
    jj = lax.broadcasted_iota(jnp.int32, (tl, tl), 1)
    same = (ii // cs) == (jj // cs)
    outs = ((qo_f_ref, mn_f_ref), (qo_b_ref, mn_b_ref))
    jobs = [(sub, d) for sub in range(n_sub) for d in range(2)]
    rows = lambda sub: slice(sub * tl, (sub + 1) * tl)
    qk = [_dot_nt(q_ref[0, rows(s), :], k_ref[0, rows(s), :]) for s in range(n_sub)]
    kk = [_dot_nt(k_ref[0, rows(s), :], k_ref[0, rows(s), :]) for s in range(n_sub)]
    gc = [gcol_ref[0, 0, rows(s), 2 * d:2 * d + 1] for s, d in jobs]
    beta = [gcol_ref[0, 0, rows(s), 2 * d + 1:2 * d + 2] for s, d in jobs]
    gr = [grow_ref[0, 0, 2 * d:2 * d + 1, rows(s)] for s, d in jobs]
    incl = [jnp.logical_and(same, (ii >= jj) if d == 0 else (ii <= jj)) for s, d in jobs]
    strict = [jnp.logical_and(same, (ii > jj) if d == 0 else (ii < jj)) for s, d in jobs]
    decay = [jnp.where(m, jnp.exp(jnp.where(m, a - b, 0.0)), 0.0) for m, a, b in zip(incl, gc, gr)]
    pw = [jnp.where(m, kk[s] * dc, 0.0) * (-bt) for (s, d), m, dc, bt in zip(jobs, strict, decay, beta)]
    acc = list(pw)
    step = 2
    while step < cs:
        p16 = [p.astype(BF16) for p in pw]
        pw = [_dot(p, p) for p in p16]
        acc = [a + p + _dot(a.astype(BF16), p.astype(BF16)) for a, p in zip(acc, pw)]
        step *= 2
    e_gc = [jnp.exp(g) for g in gc]
    rhs = [jnp.concatenate([v_ref[0, rows(s), :].astype(F32) * bt, k_ref[0, rows(s), :].astype(F32) * (bt * eg)], axis=-1)
           for (s, d), bt, eg in zip(jobs, beta, e_gc)]
    uw16 = [(r + _dot(a.astype(BF16), r.astype(BF16))).astype(BF16) for r, a in zip(rhs, acc)]
    au_aw = [_dot((qk[s] * dc).astype(BF16), x) for (s, d), dc, x in zip(jobs, decay, uw16)]
    for (s, d), eg, aa in zip(jobs, e_gc, au_aw):
        q = q_ref[0, rows(s), :].astype(F32)
        outs[d][0][0, 0, rows(s), :] = jnp.concatenate([q * eg - aa[:, hd:], aa[:, :hd]], axis=-1).astype(BF16)
    for (s, d), g, r, x in zip(jobs, gc, gr, uw16):
        edge = jnp.logical_and(same, (jj % cs) == (cs - 1 if d == 0 else 0))
        g_last = jnp.sum(jnp.where(edge, r, 0.0), axis=-1, keepdims=True)
        kd16 = (k_ref[0, rows(s), :].astype(F32) * jnp.exp(g_last - g)).astype(BF16)
        for c in range(tl // cs):
            rs = slice(c * cs, (c + 1) * cs)
            ku_kw = _dot_tn(kd16[rs], x[rs])
            m0 = (s * (tl // cs) + c) * hd
            outs[d][1][0, 0, m0:m0 + hd, :] = jnp.concatenate([-ku_kw[:, hd:], ku_kw[:, :hd]], axis=-1).astype(BF16)


def _gdn_local(qkv, gcol, grow):
    bsz, l, _ = qkv.shape
    nh = GDN_HEADS
    hd = HEAD_DIM
    tl = min(l, GDN_LOCAL_ROWS)
    per_tile = tl // GDN_CHUNK * hd
    col = lambda off: pl.BlockSpec((1, tl, hd), lambda b, h, t: (b, t, off + h))
    qo = pl.BlockSpec((1, 1, tl, 2 * hd), lambda b, h, t: (b, h, t, 0))
    mn = pl.BlockSpec((1, 1, per_tile, 2 * hd), lambda b, h, t: (b, h, t, 0))
    s_qo = jax.ShapeDtypeStruct((bsz, nh, l, 2 * hd), BF16)
    s_mn = jax.ShapeDtypeStruct((bsz, nh, l // GDN_CHUNK * hd, 2 * hd), BF16)
    return pl.pallas_call(
        _gdn_local_kernel,
        grid=(bsz, nh, l // tl),
        in_specs=[col(0), col(nh), col(2 * nh),
                  pl.BlockSpec((1, 1, tl, 4), lambda b, h, t: (b, h, t, 0)),
                  pl.BlockSpec((1, 1, 4, tl), lambda b, h, t: (b, h, 0, t))],
        out_specs=[qo, mn, qo, mn],
        out_shape=[s_qo, s_mn, s_qo, s_mn],
        compiler_params=_cparams("parallel", "parallel", "parallel"),
        name="gdn_local",
    )(qkv, qkv, qkv, gcol, grow)


GDN_SEQ_REFS = 6


def _gdn_scan_kernel(ctx_out, *refs):
    lat = refs[0:GDN_SEQ_REFS]
    ctx = refs[GDN_SEQ_REFS:2 * GDN_SEQ_REFS]
    nw_ref = refs[2 * GDN_SEQ_REFS]
    if ctx_out:
        o_ref, oc_ref, of_ref, ob_ref, ocf_ref, ocb_ref = refs[2 * GDN_SEQ_REFS + 1:]
    else:
        o_ref, of_ref, ob_ref = refs[2 * GDN_SEQ_REFS + 1:]
        oc_ref = ocf_ref = ocb_ref = None
    cs = GDN_CHUNK
    hd = HEAD_DIM

    def scan(seq_refs, out_f, out_b, state):
        qo_f, mn_f, qo_b, mn_b, gcol, _ = seq_refs
        n = qo_f.shape[2] // cs

        def one(qo, mn, c, g_row, g_lane, s, out):
            s16 = s.astype(BF16)
            r0 = pl.multiple_of(c * cs, cs)
            m0 = pl.multiple_of(c * hd, hd)
            if out is not None:
                out[pl.ds(r0, cs), :] = (_dot(qo[0, 0, pl.ds(r0, cs), 0:hd], s16)
                                         + qo[0, 0, pl.ds(r0, cs), hd:2 * hd].astype(F32))
            e_last = jnp.exp(gcol[0, 0, pl.ds(g_row, 1), g_lane:g_lane + 1])
            return (s * e_last + _dot(mn[0, 0, pl.ds(m0, hd), 0:hd], s16)
                    + mn[0, 0, pl.ds(m0, hd), hd:2 * hd].astype(F32))

        def step(i, st):
            s_f, s_b = st
            cb = n - 1 - i
            s_f = one(qo_f, mn_f, i, i * cs + cs - 1, 0, s_f, out_f)
            s_b = one(qo_b, mn_b, cb, cb * cs, 2, s_b, out_b)
            return s_f, s_b

        return lax.fori_loop(0, n, step, state)

    def finish(seq_refs, out_f, out_b, dst):
        z = seq_refs[GDN_SEQ_REFS - 1][0].astype(F32)
        o = _rms(out_f[...] + out_b[...], nw_ref[...])
        dst[0] = (o * _silu(z)).astype(dst.dtype)

    zero = jnp.zeros((hd, hd), F32)
    state = scan(ctx, ocf_ref, ocb_ref, (zero, zero))
    scan(lat, of_ref, ob_ref, state)
    finish(lat, of_ref, ob_ref, o_ref)
    if ctx_out:
        finish(ctx, ocf_ref, ocb_ref, oc_ref)


def _gdn_scan(loc_l, gcol_l, proj_l, loc_c, gcol_c, proj_c, norm_w, ctx_out):
    bsz, nh, seq, _ = loc_l[0].shape
    n_ctx = loc_c[0].shape[2]
    hd = HEAD_DIM
    z_col = Z_COL // hd

    def specs(l):
        full = lambda r, w: pl.BlockSpec((1, 1, r, w), lambda b, h: (b, h, 0, 0))
        one_dir = [full(l, 2 * hd), full(l // GDN_CHUNK * hd, 2 * hd)]
        return one_dir * 2 + [full(l, 4), pl.BlockSpec((1, l, hd), lambda b, h: (b, 0, z_col + h))]

    out_shape = [jax.ShapeDtypeStruct((bsz, seq, nh * hd), BF16)]
    out_specs = [pl.BlockSpec((1, seq, hd), lambda b, h: (b, 0, h))]
    scratch = [pltpu.VMEM((seq, hd), F32), pltpu.VMEM((seq, hd), F32)]
    if ctx_out:
        out_shape.append(jax.ShapeDtypeStruct((bsz, n_ctx, nh * hd), BF16))
        out_specs.append(pl.BlockSpec((1, n_ctx, hd), lambda b, h: (b, 0, h)))
        scratch += [pltpu.VMEM((n_ctx, hd), F32), pltpu.VMEM((n_ctx, hd), F32)]
    res = pl.pallas_call(
        functools.partial(_gdn_scan_kernel, ctx_out),
        grid=(bsz, nh),
        in_specs=specs(seq) + specs(n_ctx) + [pl.BlockSpec((1, hd), lambda b, h: (0, 0))],
        out_specs=out_specs,
        out_shape=out_shape,
        scratch_shapes=scratch,
        compiler_params=_cparams("parallel", "parallel"),
        name="gdn_scan",
    )(*loc_l, gcol_l, proj_l, *loc_c, gcol_c, proj_c, norm_w.reshape(1, hd))
    return (res[0], res[1]) if ctx_out else (res[0], None)


def _gate_layouts(gb):
    bsz, l, _ = gb.shape
    nh = GDN_HEADS
    t = gb[:, :, :4 * nh].reshape(bsz, l, 2, 2, nh)
    gcol = jnp.transpose(t, (0, 4, 1, 2, 3)).reshape(bsz, nh, l, 4)
    grow = jnp.transpose(t, (0, 4, 2, 3, 1)).reshape(bsz, nh, 4, l)
    return gcol, grow


NA_WIDTH = NA_HEADS * HEAD_DIM
POOL_WIDTH = LANES * len(POOL_WINDOWS)
GDN_WIDTH = GDN_HEADS * HEAD_DIM
POOL_COL = 0
NA_COL = POOL_WIDTH
GDN_COL = NA_COL + 3 * NA_WIDTH
Z_COL = GDN_COL + 3 * GDN_WIDTH
AB_COL = Z_COL + GDN_WIDTH


def _reorder_in_proj(w):
    na = 3 * NA_WIDTH
    return jnp.concatenate([w[:, na:na + POOL_WIDTH], w[:, :na], w[:, na + POOL_WIDTH:]], axis=1)


def _token_mixers(proj_l, ab_l, proj_c, ab_c, lw, ctx_out):
    y_na_l, y_na_c = _na(proj_l, proj_c, lw["na_bias"], lw["na_q_norm"], lw["na_k_norm"], ctx_out)
    y_pool_l = _pool(proj_l, lw["bands"], lw["pool_w"], lw["pool_scale"])
    y_pool_c = _pool(proj_c, lw["bands"], lw["pool_w"], lw["pool_scale"]) if ctx_out else None

    def local(proj, ab):
        gb = _gdn_gates(ab, lw["gdn_a_log"], lw["gdn_dt_bias"])
        gcol, grow = _gate_layouts(gb)
        return _gdn_local(_gdn_prep(proj, lw["gdn_conv"]), gcol, grow), gcol

    loc_l, gcol_l = local(proj_l, ab_l)
    loc_c, gcol_c = local(proj_c, ab_c)
    y_gdn_l, y_gdn_c = _gdn_scan(loc_l, gcol_l, proj_l, loc_c, gcol_c, proj_c, lw["gdn_norm"], ctx_out)
    return (y_na_l, y_pool_l, y_gdn_l), (y_na_c, y_pool_c, y_gdn_c)


def kernel(x, c, ctx, c_ctx, w_mod, b_mod, w_in, w_out, na_q_norm, na_k_norm, na_rpb, pool_w, pool_scale,
           gdn_conv, gdn_a_log, gdn_dt_bias, gdn_norm, ffn_w_gate, ffn_w_up, ffn_w_down,
           moe_router, moe_w_gate, moe_w_up, moe_w_down):
    bsz, seq, d = x.shape
    n_ctx = ctx.shape[1]
    depth = w_mod.shape[0]
    assert seq % (GRID_W * NA_Q_ROWS * NA_GROUPS_PER_STEP) == 0 and seq // GRID_W >= NA_K_ROWS
    assert seq % SEQ_TILE == 0 and n_ctx % SEQ_TILE == 0 and w_in.shape[2] == AB_COL + 4 * GDN_HEADS

    mod_rows = 16
    c_rows = jnp.zeros((mod_rows, d), F32).at[:bsz].set(c).at[bsz].set(c_ctx)
    mods = _modulation(c_rows, w_mod, b_mod).reshape(depth, mod_rows, 6, d)
    bands = _pool_bands()

    xl = x
    xc = ctx.reshape(1, bsz * n_ctx, d)
    for i in range(depth):
        last = i == depth - 1
        mod_l = mods[i, :bsz]
        mod_c = mods[i, bsz:bsz + 1]
        w_main = _reorder_in_proj(w_in[i, :, :AB_COL]).astype(BF16)
        w_ab = jnp.pad(w_in[i, :, AB_COL:], ((0, 0), (0, LANES - 4 * GDN_HEADS))).astype(BF16)
        wo = w_out[i].astype(BF16)
        ws = (wo[:NA_WIDTH], wo[NA_WIDTH:NA_WIDTH + POOL_WIDTH], wo[NA_WIDTH + POOL_WIDTH:])
        lw = dict(na_bias=_na_bias_table(na_rpb[i], seq // GRID_W), na_q_norm=na_q_norm[i], na_k_norm=na_k_norm[i],
                  bands=bands, pool_w=pool_w[i].astype(BF16), pool_scale=pool_scale[i], gdn_conv=gdn_conv[i],
                  gdn_a_log=gdn_a_log[i], gdn_dt_bias=gdn_dt_bias[i], gdn_norm=gdn_norm[i])

        h_l = _norm_mod(xl, mod_l, 0).reshape(bsz * seq, d)
        h_c = _norm_mod(xc, mod_c, 0).reshape(bsz * n_ctx, d)
        half = AB_COL // 2
        proj_l = _mm(h_l, w_main, BF16, TOK_TILE, half).reshape(bsz, seq, AB_COL)
        proj_c = _mm(h_c, w_main, BF16, TOK_TILE, half).reshape(bsz, n_ctx, AB_COL)
        ab_l = _mm(h_l, w_ab, F32, 2048, LANES).reshape(bsz, seq, LANES)
        ab_c = _mm(h_c, w_ab, F32, 2048, LANES).reshape(bsz, n_ctx, LANES)
        ys_l, ys_c = _token_mixers(proj_l, ab_l, proj_c, ab_c, lw, not last)
        xl = _out_proj(xl, mod_l, 2, ys_l, ws)
        if not last:
            xc = _out_proj(xc, mod_c, 2, [y.reshape(1, bsz * n_ctx, -1) for y in ys_c], ws)

        streams = [(xl, mod_l)] if last else [(xl, mod_l), (xc, mod_c)]
        outs = []
        j = i // 2
        for xs, mod in streams:
            if i % 2 == 0:
                h = _norm_mod(xs, mod, 3)
                outs.append(_ffn(xs, mod, 5, h, ffn_w_gate[j].astype(BF16), ffn_w_up[j].astype(BF16),
                                 ffn_w_down[j].astype(BF16)))
            else:
                h, h_packed = _norm_mod(xs, mod, 3, packed=True)
                rw = jnp.pad(moe_router[j], ((0, 0), (0, LANES - N_EXPERTS))).astype(BF16)
                outs.append(_moe(xs, mod, 5, h, h_packed, rw, moe_w_gate[j].astype(BF16), moe_w_up[j].astype(BF16),
                                 moe_w_down[j].astype(BF16)))
        xl = outs[0]
        if not last:
            xc = outs[1]
    return xl
```

```python
import functools

import numpy as np
import jax
import jax.numpy as jnp
from jax import lax
from jax.experimental import pallas as pl
from jax.experimental.pallas import tpu as pltpu

F32 = jnp.float32
BF16 = jnp.bfloat16

EPS = 1e-6
HEAD_DIM = 128
GRID_W = 64
NA_HEADS = 6
NA_WIN_R = 8
NA_WIN_C = 16
NA_Q_ROWS = 4
NA_K_ROWS = NA_Q_ROWS + NA_WIN_R
NA_GROUPS_PER_STEP = 2
POOL_WINDOWS = (2, 4, 8, 16)
GDN_HEADS = 6
GDN_CHUNK = 64
N_EXPERTS = 8
LANES = 128
MASK_VALUE = -1e30
VMEM_LIMIT = 56 * 1024 * 1024
SEQ_TILE = 256
TOK_TILE = 512
MOE_TILE = 512
GATHER_TILE = 256
GDN_LOCAL_ROWS = 1024


def _cparams(*sem):
    return pltpu.CompilerParams(dimension_semantics=sem, vmem_limit_bytes=VMEM_LIMIT)


def _silu(x):
    return x / (1.0 + jnp.exp(-x))


def _dot(a, b):
    return jnp.dot(a, b, preferred_element_type=F32)


def _dot_nt(a, b):
    return lax.dot_general(a, b, (((1,), (1,)), ((), ())), preferred_element_type=F32)


def _dot_tn(a, b):
    return lax.dot_general(a, b, (((0,), (0,)), ((), ())), preferred_element_type=F32)


def _rms(x, w):
    xf = x.astype(F32)
    return xf * lax.rsqrt(jnp.mean(xf * xf, axis=-1, keepdims=True) + EPS) * w


def _mod_kernel(a_ref, w_ref, b_ref, o_ref):
    a = _silu(a_ref[...])
    o_ref[0] = _dot(a.astype(BF16), w_ref[0].astype(BF16)) + b_ref[0]


def _modulation(c_rows, w_mod, b_mod):
    depth, d, n = w_mod.shape
    r = c_rows.shape[0]
    tn = 1024
    return pl.pallas_call(
        _mod_kernel,
        grid=(depth, n // tn),
        in_specs=[pl.BlockSpec((r, d), lambda i, j: (0, 0)),
                  pl.BlockSpec((1, d, tn), lambda i, j: (i, 0, j)),
                  pl.BlockSpec((1, 1, tn), lambda i, j: (i, 0, j))],
        out_specs=pl.BlockSpec((1, r, tn), lambda i, j: (i, 0, j)),
        out_shape=jax.ShapeDtypeStruct((depth, r, n), F32),
        compiler_params=_cparams("parallel", "parallel"),
        name="modulation",
    )(c_rows, w_mod, b_mod.reshape(depth, 1, n))


HIGH_HALF = 0xFFFF0000


def _pack_bf16_pairs(h):
    half = h.shape[1] // 2
    hf = h.astype(F32)
    lo = lax.bitcast_convert_type(hf[:, :half], jnp.uint32) >> 16
    hi = lax.bitcast_convert_type(hf[:, half:], jnp.uint32) & jnp.uint32(HIGH_HALF)
    return hi | lo


def _unpack_bf16_pairs(w):
    lo = lax.bitcast_convert_type(w << 16, F32).astype(BF16)
    hi = lax.bitcast_convert_type(w & jnp.uint32(HIGH_HALF), F32).astype(BF16)
    return lo, hi


def _norm_mod_kernel(si, x_ref, m_ref, o_ref, *packed_ref):
    x = x_ref[0]
    y = x * lax.rsqrt(jnp.mean(x * x, axis=-1, keepdims=True) + EPS)
    h = (y * (1.0 + m_ref[0, si + 1:si + 2, :]) + m_ref[0, si:si + 1, :]).astype(BF16)
    o_ref[0] = h
    if packed_ref:
        packed_ref[0][0] = _pack_bf16_pairs(h)


def _norm_mod(x, mod, si, packed=False):
    g, l, d = x.shape
    tl = min(l, TOK_TILE)
    out_shape = [jax.ShapeDtypeStruct((g, l, d), BF16)]
    out_specs = [pl.BlockSpec((1, tl, d), lambda b, t: (b, t, 0))]
    if packed:
        out_shape.append(jax.ShapeDtypeStruct((g, l, d // 2), jnp.uint32))
        out_specs.append(pl.BlockSpec((1, tl, d // 2), lambda b, t: (b, t, 0)))
    res = pl.pallas_call(
        functools.partial(_norm_mod_kernel, si),
        grid=(g, l // tl),
        in_specs=[pl.BlockSpec((1, tl, d), lambda b, t: (b, t, 0)),
                  pl.BlockSpec((1, 6, d), lambda b, t: (b, 0, 0))],
        out_specs=out_specs,
        out_shape=out_shape,
        compiler_params=_cparams("parallel", "parallel"),
        name="norm_mod",
    )(x, mod)
    return tuple(res) if packed else res[0]


def _mm_kernel(a_ref, w_ref, o_ref):
    o_ref[...] = _dot(a_ref[...], w_ref[...]).astype(o_ref.dtype)


def _mm(a, w, out_dtype, tm, tn):
    m, k = a.shape
    n = w.shape[1]
    tm = min(tm, m)
    return pl.pallas_call(
        _mm_kernel,
        grid=(n // tn, m // tm),
        in_specs=[pl.BlockSpec((tm, k), lambda j, i: (i, 0)),
                  pl.BlockSpec((k, tn), lambda j, i: (0, j))],
        out_specs=pl.BlockSpec((tm, tn), lambda j, i: (i, j)),
        out_shape=jax.ShapeDtypeStruct((m, n), out_dtype),
        compiler_params=_cparams("parallel", "parallel"),
        name="matmul",
    )(a, w)


def _out_proj_kernel(gi, x_ref, m_ref, y0_ref, y1_ref, y2_ref, w0_ref, w1_ref, w2_ref, o_ref):
    acc = _dot(y0_ref[0], w0_ref[...]) + _dot(y1_ref[0], w1_ref[...]) + _dot(y2_ref[0], w2_ref[...])
    o_ref[0] = x_ref[0] + m_ref[0, gi:gi + 1, :] * acc


def _out_proj(x, mod, gi, ys, ws):
    g, l, d = x.shape
    tl = min(l, TOK_TILE)
    y_specs = [pl.BlockSpec((1, tl, y.shape[2]), lambda b, t: (b, t, 0)) for y in ys]
    w_specs = [pl.BlockSpec(w.shape, lambda b, t: (0, 0)) for w in ws]
    return pl.pallas_call(
        functools.partial(_out_proj_kernel, gi),
        grid=(g, l // tl),
        in_specs=[pl.BlockSpec((1, tl, d), lambda b, t: (b, t, 0)),
                  pl.BlockSpec((1, 6, d), lambda b, t: (b, 0, 0))] + y_specs + w_specs,
        out_specs=pl.BlockSpec((1, tl, d), lambda b, t: (b, t, 0)),
        out_shape=jax.ShapeDtypeStruct((g, l, d), F32),
        compiler_params=_cparams("parallel", "parallel"),
        name="out_proj",
    )(x, mod, *ys, *ws)


def _ffn_kernel(gi, x_ref, m_ref, h_ref, wg_ref, wu_ref, wd_ref, o_ref, acc_ref):
    f = pl.program_id(2)

    @pl.when(f == 0)
    def _():
        acc_ref[...] = jnp.zeros_like(acc_ref)

    h = h_ref[0]
    a = _silu(_dot(h, wg_ref[...])) * _dot(h, wu_ref[...])
    acc_ref[...] += _dot(a.astype(BF16), wd_ref[...])

    @pl.when(f == pl.num_programs(2) - 1)
    def _():
        o_ref[0] = x_ref[0] + m_ref[0, gi:gi + 1, :] * acc_ref[...]


def _ffn(x, mod, gi, h, wg, wu, wd):
    g, l, d = x.shape
    dff = wg.shape[1]
    tl = min(l, TOK_TILE)
    tf = 512
    return pl.pallas_call(
        functools.partial(_ffn_kernel, gi),
        grid=(g, l // tl, dff // tf),
        in_specs=[pl.BlockSpec((1, tl, d), lambda b, t, f: (b, t, 0)),
                  pl.BlockSpec((1, 6, d), lambda b, t, f: (b, 0, 0)),
                  pl.BlockSpec((1, tl, d), lambda b, t, f: (b, t, 0)),
                  pl.BlockSpec((d, tf), lambda b, t, f: (0, f)),
                  pl.BlockSpec((d, tf), lambda b, t, f: (0, f)),
                  pl.BlockSpec((tf, d), lambda b, t, f: (f, 0))],
        out_specs=pl.BlockSpec((1, tl, d), lambda b, t, f: (b, t, 0)),
        out_shape=jax.ShapeDtypeStruct((g, l, d), F32),
        scratch_shapes=[pltpu.VMEM((tl, d), F32)],
        compiler_params=_cparams("parallel", "parallel", "arbitrary"),
        name="ffn",
    )(x, mod, h, wg, wu, wd)


def _router_kernel(h_ref, w_ref, o_ref):
    logits = _dot(h_ref[0], w_ref[...])
    lane = lax.broadcasted_iota(jnp.int32, logits.shape, 1).astype(F32)
    l1 = jnp.where(lane < N_EXPERTS, logits, -jnp.inf)
    m1 = jnp.max(l1, axis=-1, keepdims=True)
    i1 = jnp.min(jnp.where(l1 == m1, lane, float(LANES)), axis=-1, keepdims=True)
    l2 = jnp.where(lane == i1, -jnp.inf, l1)
    m2 = jnp.max(l2, axis=-1, keepdims=True)
    i2 = jnp.min(jnp.where(l2 == m2, lane, float(LANES)), axis=-1, keepdims=True)
    e2 = jnp.exp(m2 - m1)
    w1 = 1.0 / (1.0 + e2)
    w2 = e2 / (1.0 + e2)
    o_ref[0] = jnp.where(lane == 0, i1, jnp.where(lane == 1, i2, jnp.where(lane == 2, w1, jnp.where(lane == 3, w2, 0.0))))


def _router(h, w_pad):
    g, l, d = h.shape
    tl = min(l, TOK_TILE)
    return pl.pallas_call(
        _router_kernel,
        grid=(g, l // tl),
        in_specs=[pl.BlockSpec((1, tl, d), lambda b, t: (b, t, 0)),
                  pl.BlockSpec((d, LANES), lambda b, t: (0, 0))],
        out_specs=pl.BlockSpec((1, tl, LANES), lambda b, t: (b, t, 0)),
        out_shape=jax.ShapeDtypeStruct((g, l, LANES), F32),
        compiler_params=_cparams("parallel", "parallel"),
        name="router",
    )(h, w_pad)


def _dispatch_plan(info):
    t = info.shape[0] * info.shape[1]
    flat = info.reshape(t, LANES)
    e = jnp.concatenate([flat[:, 0], flat[:, 1]]).astype(jnp.int32)
    onehot = (e[:, None] == jnp.arange(N_EXPERTS, dtype=jnp.int32)[None, :]).astype(jnp.int32)
    csum = jnp.cumsum(onehot, axis=0)
    rank = jnp.sum(csum * onehot, axis=1) - 1
    padded = (csum[-1] + MOE_TILE - 1) // MOE_TILE * MOE_TILE
    ends = jnp.cumsum(padded)
    pos = jnp.sum(onehot * (ends - padded)[None, :], axis=1) + rank
    n_rows = 2 * t + N_EXPERTS * MOE_TILE
    tok = jnp.tile(jnp.arange(t, dtype=jnp.int32), 2)
    row_token = jnp.zeros((n_rows,), jnp.int32).at[pos].set(tok)
    tile_start = jnp.arange(n_rows // MOE_TILE, dtype=jnp.int32) * MOE_TILE
    tile_expert = jnp.minimum(jnp.sum((ends[None, :] <= tile_start[:, None]).astype(jnp.int32), axis=1), N_EXPERTS - 1)
    n_used = (ends[-1] // MOE_TILE).astype(jnp.int32).reshape(1)
    return row_token, tile_expert, n_used, pos[:t], pos[t:]


def _row_copy(src_ref, dst_ref, sem, src_row, dst_row):
    return pltpu.make_async_copy(src_ref.at[pl.ds(src_row, 1)], dst_ref.at[pl.ds(dst_row, 1)], sem)


def _gather_kernel(idx_ref, src_ref, o_ref, sem):
    n = o_ref.shape[0]

    def issue(r, carry):
        _row_copy(src_ref, o_ref, sem, idx_ref[0, 0, r], r).start()
        return carry

    def wait(r, carry):
        _row_copy(src_ref, o_ref, sem, 0, 0).wait()
        return carry

    lax.fori_loop(0, n, issue, 0, unroll=8)
    lax.fori_loop(0, n, wait, 0, unroll=8)


def _gather_rows(src, idx):
    n = idx.shape[0]
    w = src.shape[1]
    tg = MOE_TILE
    return pl.pallas_call(
        _gather_kernel,
        grid=(n // tg,),
        in_specs=[pl.BlockSpec((1, 1, tg), lambda i: (i, 0, 0), memory_space=pltpu.SMEM),
                  pl.BlockSpec(memory_space=pl.ANY)],
        out_specs=pl.BlockSpec((tg, w), lambda i: (i, 0)),
        out_shape=jax.ShapeDtypeStruct((n, w), src.dtype),
        scratch_shapes=[pltpu.SemaphoreType.DMA(())],
        compiler_params=_cparams("arbitrary"),
        name="moe_gather",
    )(idx.reshape(n // tg, 1, tg), src)


def _experts_kernel(te_ref, nu_ref, hp_ref, wg_ref, wu_ref, wd_ref, o_ref, acc_ref):
    i = pl.program_id(0)
    f = pl.program_id(1)
    nf = pl.num_programs(1)
    used = i < nu_ref[0]

    @pl.when(used)
    def _():
        @pl.when(f == 0)
        def _():
            acc_ref[...] = jnp.zeros_like(acc_ref)

        lo, hi = _unpack_bf16_pairs(hp_ref[...])
        half = lo.shape[1]
        g = _dot(lo, wg_ref[0, :half, :]) + _dot(hi, wg_ref[0, half:, :])
        u = _dot(lo, wu_ref[0, :half, :]) + _dot(hi, wu_ref[0, half:, :])
        acc_ref[...] += _dot((_silu(g) * u).astype(BF16), wd_ref[0])

        @pl.when(f == nf - 1)
        def _():
            o_ref[...] = acc_ref[...]

    @pl.when(jnp.logical_and(jnp.logical_not(used), f == nf - 1))
    def _():
        o_ref[...] = jnp.zeros_like(o_ref)


def _experts(h_sorted, tile_expert, n_used, wg, wu, wd):
    n, half = h_sorted.shape
    d = 2 * half
    dff = wg.shape[2]
    tm = MOE_TILE
    tf = min(dff, 1024)
    grid_spec = pltpu.PrefetchScalarGridSpec(
        num_scalar_prefetch=2,
        grid=(n // tm, dff // tf),
        in_specs=[pl.BlockSpec((tm, half), lambda i, f, te, nu: (i, 0)),
                  pl.BlockSpec((1, d, tf), lambda i, f, te, nu: (te[i], 0, f)),
                  pl.BlockSpec((1, d, tf), lambda i, f, te, nu: (te[i], 0, f)),
                  pl.BlockSpec((1, tf, d), lambda i, f, te, nu: (te[i], f, 0))],
        out_specs=pl.BlockSpec((tm, d), lambda i, f, te, nu: (i, 0)),
        scratch_shapes=[pltpu.VMEM((tm, d), F32)],
    )
    return pl.pallas_call(
        _experts_kernel,
        grid_spec=grid_spec,
        out_shape=jax.ShapeDtypeStruct((n, d), F32),
        compiler_params=_cparams("arbitrary", "arbitrary"),
        name="moe_experts",
    )(tile_expert, n_used, h_sorted, wg, wu, wd)


def _combine_kernel(gi, p1_ref, p2_ref, x_ref, m_ref, info_ref, y_ref, o_ref, a_ref, b_ref, sem):
    n = x_ref.shape[1]

    def issue(r, carry):
        _row_copy(y_ref, a_ref, sem.at[0], p1_ref[0, 0, r], r).start()
        _row_copy(y_ref, b_ref, sem.at[1], p2_ref[0, 0, r], r).start()
        return carry

    def wait(r, carry):
        _row_copy(y_ref, a_ref, sem.at[0], 0, 0).wait()
        _row_copy(y_ref, b_ref, sem.at[1], 0, 0).wait()
        return carry

    lax.fori_loop(0, n, issue, 0, unroll=8)
    lax.fori_loop(0, n, wait, 0, unroll=8)
    info = info_ref[0]
    mix = info[:, 2:3] * a_ref[...] + info[:, 3:4] * b_ref[...]
    o_ref[0] = x_ref[0] + m_ref[0, gi:gi + 1, :] * mix


def _combine(x, mod, gi, info, y_sorted, pos1, pos2):
    g, l, d = x.shape
    tc = GATHER_TILE
    nt = l // tc
    idx_spec = pl.BlockSpec((1, 1, tc), lambda b, t: (b * nt + t, 0, 0), memory_space=pltpu.SMEM)
    return pl.pallas_call(
        functools.partial(_combine_kernel, gi),
        grid=(g, nt),
        in_specs=[idx_spec, idx_spec,
                  pl.BlockSpec((1, tc, d), lambda b, t: (b, t, 0)),
                  pl.BlockSpec((1, 6, d), lambda b, t: (b, 0, 0)),
                  pl.BlockSpec((1, tc, LANES), lambda b, t: (b, t, 0)),
                  pl.BlockSpec(memory_space=pl.ANY)],
        out_specs=pl.BlockSpec((1, tc, d), lambda b, t: (b, t, 0)),
        out_shape=jax.ShapeDtypeStruct((g, l, d), F32),
        scratch_shapes=[pltpu.VMEM((tc, d), F32), pltpu.VMEM((tc, d), F32), pltpu.SemaphoreType.DMA((2,))],
        compiler_params=_cparams("arbitrary", "arbitrary"),
        name="moe_combine",
    )(pos1.reshape(g * nt, 1, tc), pos2.reshape(g * nt, 1, tc), x, mod, info, y_sorted)


def _moe(x, mod, gi, h, h_packed, router_w, wg, wu, wd):
    g, l, d = x.shape
    info = _router(h, router_w)
    row_token, tile_expert, n_used, pos1, pos2 = _dispatch_plan(info)
    h_sorted = _gather_rows(h_packed.reshape(g * l, d // 2), row_token)
    y_sorted = _experts(h_sorted, tile_expert, n_used, wg, wu, wd)
    return _combine(x, mod, gi, info, y_sorted, pos1, pos2)


def _na_bias_table(rpb, rows):
    w = GRID_W
    col = np.arange(w)
    c0 = np.clip(col - NA_WIN_C // 2, 0, w - NA_WIN_C)
    col_in = (col[None, :] >= c0[:, None]) & (col[None, :] < c0[:, None] + NA_WIN_C)
    col_idx = np.clip(col[None, :] - col[:, None], 1 - NA_WIN_C, NA_WIN_C - 1) + NA_WIN_C - 1
    n_groups = rows // NA_Q_ROWS
    onehot = (col_idx[None] == np.arange(2 * NA_WIN_C - 1)[:, None, None]).astype(np.float32)
    blocks = jnp.einsum('hdc,cqk->hdqk', rpb.astype(F32), jnp.asarray(onehot), precision=lax.Precision.HIGHEST)
    blocks = jnp.where(col_in[None, None], blocks, MASK_VALUE)
    masked = jnp.full_like(blocks[:, 0], MASK_VALUE)
    variants = []
    for grp in (0, 1, n_groups - 1):
        r = grp * NA_Q_ROWS
        start = _na_key_start(r, rows)
        q_rows = []
        for a in range(NA_Q_ROWS):
            ws = int(np.clip(r + a - NA_WIN_R // 2, 0, rows - NA_WIN_R))
            row = []
            for i in range(NA_K_ROWS):
                kr = start + i
                row.append(blocks[:, kr - (r + a) + NA_WIN_R - 1] if ws <= kr < ws + NA_WIN_R else masked)
            q_rows.append(jnp.concatenate(row, axis=-1))
        variants.append(jnp.concatenate(q_rows, axis=-2))
    return jnp.stack(variants, axis=1)


def _na_key_start(r, rows):
    return int(np.clip(r - NA_WIN_R // 2, 0, rows - NA_K_ROWS))


def _na_kernel(ctx_out, q_ref, k_ref, v_ref, qc_ref, kc_ref, vc_ref, bias_ref, qw_ref, kw_ref, *rest):
    if ctx_out:
        o_ref, oc_ref, kn_ref = rest
    else:
        o_ref, kn_ref = rest
    n_ctx = kc_ref.shape[1]
    seq = k_ref.shape[1]
    rows = seq // GRID_W
    n_groups = rows // NA_Q_ROWS
    nq = NA_Q_ROWS * GRID_W
    nk = NA_K_ROWS * GRID_W
    scale = HEAD_DIM ** -0.5
    qw = qw_ref[...]
    kw = kw_ref[...]
    kn_ref[0:n_ctx, :] = _rms(kc_ref[0], kw).astype(BF16)

    def norm_keys(j, carry):
        r0 = pl.multiple_of(j * nq, nq)
        kn_ref[pl.ds(n_ctx + r0, nq), :] = _rms(k_ref[0, pl.ds(r0, nq), :], kw).astype(BF16)
        return carry

    lax.fori_loop(0, seq // nq, norm_keys, 0)
    k_ctx = kn_ref[0:n_ctx, :]
    v_ctx = vc_ref[0]

    def groups(it, carry):
        gs = [it * NA_GROUPS_PER_STEP + u for u in range(NA_GROUPS_PER_STEP)]
        q0 = [pl.multiple_of(g * nq, nq) for g in gs]
        k0 = [pl.multiple_of(jnp.clip(g * NA_Q_ROWS - NA_WIN_R // 2, 0, rows - NA_K_ROWS) * GRID_W, GRID_W) for g in gs]
        variant = [jnp.where(g == 0, 0, jnp.where(g == n_groups - 1, 2, 1)) for g in gs]
        qn = [(_rms(q_ref[0, pl.ds(q, nq), :], qw) * scale).astype(BF16) for q in q0]
        s = [_dot_nt(a, kn_ref[pl.ds(n_ctx + k, nk), :]) + bias_ref[0, var] for a, k, var in zip(qn, k0, variant)]
        sc = [_dot_nt(a, k_ctx) for a in qn]
        m = [jnp.maximum(jnp.max(a, axis=-1, keepdims=True), jnp.max(b, axis=-1, keepdims=True)) for a, b in zip(s, sc)]
        p = [jnp.exp(a - mm) for a, mm in zip(s, m)]
        pc = [jnp.exp(a - mm) for a, mm in zip(sc, m)]
        denom = [jnp.sum(a, axis=-1, keepdims=True) + jnp.sum(b, axis=-1, keepdims=True) for a, b in zip(p, pc)]
        o = [_dot(a.astype(BF16), v_ref[0, pl.ds(k, nk), :]) + _dot(b.astype(BF16), v_ctx) for a, b, k in zip(p, pc, k0)]
        for q, oo, dd in zip(q0, o, denom):
            o_ref[0, pl.ds(q, nq), :] = (oo / dd).astype(o_ref.dtype)
        return carry

    lax.fori_loop(0, n_groups // NA_GROUPS_PER_STEP, groups, 0)

    if ctx_out:
        qn = (_rms(qc_ref[0], qw) * scale).astype(BF16)
        sc = _dot_nt(qn, k_ctx)
        pc = jnp.exp(sc - jnp.max(sc, axis=-1, keepdims=True))
        o = _dot(pc.astype(BF16), v_ctx) / jnp.sum(pc, axis=-1, keepdims=True)
        oc_ref[0] = o.astype(oc_ref.dtype)


def _na(proj_l, proj_c, bias, q_norm, k_norm, ctx_out):
    bsz, seq, _ = proj_l.shape
    n_ctx = proj_c.shape[1]
    h_ = NA_HEADS
    hd = HEAD_DIM
    c0 = NA_COL // hd
    lat = lambda off: pl.BlockSpec((1, seq, hd), lambda h, b: (b, 0, c0 + off + h))
    ctx = lambda off: pl.BlockSpec((1, n_ctx, hd), lambda h, b: (b, 0, c0 + off + h))
    out_shape = [jax.ShapeDtypeStruct((bsz, seq, h_ * hd), BF16)]
    out_specs = [pl.BlockSpec((1, seq, hd), lambda h, b: (b, 0, h))]
    if ctx_out:
        out_shape.append(jax.ShapeDtypeStruct((bsz, n_ctx, h_ * hd), BF16))
        out_specs.append(pl.BlockSpec((1, n_ctx, hd), lambda h, b: (b, 0, h)))
    res = pl.pallas_call(
        functools.partial(_na_kernel, ctx_out),
        grid=(h_, bsz),
        in_specs=[lat(0), lat(h_), lat(2 * h_), ctx(0), ctx(h_), ctx(2 * h_),
                  pl.BlockSpec((1,) + bias.shape[1:], lambda h, b: (h, 0, 0, 0)),
                  pl.BlockSpec((1, hd), lambda h, b: (0, 0)),
                  pl.BlockSpec((1, hd), lambda h, b: (0, 0))],
        out_specs=out_specs,
        out_shape=out_shape,
        scratch_shapes=[pltpu.VMEM((n_ctx + seq, hd), BF16)],
        compiler_params=_cparams("parallel", "parallel"),
        name="neighbourhood_attention",
    )(proj_l, proj_l, proj_l, proj_c, proj_c, proj_c, bias, q_norm.reshape(1, hd), k_norm.reshape(1, hd))
    return (res[0], res[1]) if ctx_out else (res[0], None)


def _pool_bands():
    t = SEQ_TILE
    i = np.arange(t)[:, None]
    j = np.arange(t)[None, :]
    bands = np.zeros((len(POOL_WINDOWS), 3, t, t), np.float32)
    for gi, win in enumerate(POOL_WINDOWS):
        half = win // 2
        for s, off in enumerate((-t, 0, t)):
            jj = j + off
            bands[gi, s] = (jj >= i - half) & (jj <= i + half - 1)
    return jnp.asarray(bands, BF16)


def _pool_kernel(seq_len, u_ref, up_ref, un_ref, band_ref, pw_ref, sc_ref, o_ref):
    t = pl.program_id(1)
    nt = pl.num_programs(1)
    tl = u_ref.shape[1]
    has_prev = (t > 0).astype(F32)
    has_next = (t < nt - 1).astype(F32)
    pos = t * tl + lax.broadcasted_iota(jnp.int32, (tl, 1), 0)
    for gi, win in enumerate(POOL_WINDOWS):
        cs = slice(gi * LANES, (gi + 1) * LANES)
        cur = u_ref[0, :, cs]
        wsum = (_dot(band_ref[gi, 1], cur) + has_prev * _dot(band_ref[gi, 0], up_ref[0, :, cs])
                + has_next * _dot(band_ref[gi, 2], un_ref[0, :, cs]))
        cnt = jnp.minimum(pos + win // 2, seq_len) - jnp.maximum(pos - win // 2, 0)
        d = wsum / cnt.astype(F32) - cur.astype(F32)
        o_ref[0, :, cs] = (_dot(d.astype(BF16), pw_ref[gi]) * sc_ref[:, cs]).astype(o_ref.dtype)


def _pool(proj, bands, pool_w, pool_scale):
    assert POOL_COL == 0
    bsz, l, _ = proj.shape
    tl = SEQ_TILE
    nt = l // tl
    pw = POOL_WIDTH
    return pl.pallas_call(
        functools.partial(_pool_kernel, l),
        grid=(bsz, nt),
        in_specs=[pl.BlockSpec((1, tl, pw), lambda b, t: (b, t, 0)),
                  pl.BlockSpec((1, tl, pw), lambda b, t: (b, jnp.maximum(t - 1, 0), 0)),
                  pl.BlockSpec((1, tl, pw), lambda b, t: (b, jnp.minimum(t + 1, nt - 1), 0)),
                  pl.BlockSpec(bands.shape, lambda b, t: (0, 0, 0, 0)),
                  pl.BlockSpec(pool_w.shape, lambda b, t: (0, 0, 0)),
                  pl.BlockSpec((1, pw), lambda b, t: (0, 0))],
        out_specs=pl.BlockSpec((1, tl, pw), lambda b, t: (b, t, 0)),
        out_shape=jax.ShapeDtypeStruct((bsz, l, pw), BF16),
        compiler_params=_cparams("parallel", "parallel"),
        name="pool_mixer",
    )(proj, proj, proj, bands, pool_w, pool_scale.reshape(1, pw))


def _gdn_prep_kernel(x_ref, xp_ref, xn_ref, w_ref, o_ref):
    t = pl.program_id(1)
    j = pl.program_id(2)
    tl = x_ref.shape[1]
    halo = xp_ref.shape[1]
    x = x_ref[0].astype(F32)
    row = lax.broadcasted_iota(jnp.int32, x.shape, 0)
    prev_row = jnp.where(t > 0, xp_ref[0, halo - 1:halo, :].astype(F32), 0.0)
    next_row = jnp.where(t < pl.num_programs(1) - 1, xn_ref[0, 0:1, :].astype(F32), 0.0)
    x_m1 = jnp.where(row == 0, prev_row, pltpu.roll(x, 1, axis=0))
    x_p1 = jnp.where(row == tl - 1, next_row, pltpu.roll(x, tl - 1, axis=0))
    y = _silu(w_ref[0:1, :] * x_m1 + w_ref[1:2, :] * x + w_ref[2:3, :] * x_p1)
    third = pl.num_programs(2) // 3
    for hh in range(y.shape[1] // HEAD_DIM):
        cs = slice(hh * HEAD_DIM, (hh + 1) * HEAD_DIM)
        yh = y[:, cs]
        inv = lax.rsqrt(jnp.sum(yh * yh, axis=-1, keepdims=True) + EPS)
        fac = jnp.where(j < third, inv * HEAD_DIM ** -0.5, jnp.where(j < 2 * third, inv, 1.0))
        o_ref[0, :, cs] = (yh * fac).astype(o_ref.dtype)


def _gdn_prep(proj, conv_w):
    bsz, l, _ = proj.shape
    width = conv_w.shape[1]
    tl = min(l, 4 * SEQ_TILE)
    tc = 2 * HEAD_DIM
    halo = 16
    nt = l // tl
    cb = GDN_COL // tc
    nh = l // halo
    return pl.pallas_call(
        _gdn_prep_kernel,
        grid=(bsz, nt, width // tc),
        in_specs=[pl.BlockSpec((1, tl, tc), lambda b, t, j: (b, t, cb + j)),
                  pl.BlockSpec((1, halo, tc), lambda b, t, j: (b, jnp.maximum(t * (tl // halo) - 1, 0), cb + j)),
                  pl.BlockSpec((1, halo, tc), lambda b, t, j: (b, jnp.minimum((t + 1) * (tl // halo), nh - 1), cb + j)),
                  pl.BlockSpec((3, tc), lambda b, t, j: (0, j))],
        out_specs=pl.BlockSpec((1, tl, tc), lambda b, t, j: (b, t, j)),
        out_shape=jax.ShapeDtypeStruct((bsz, l, width), BF16),
        compiler_params=_cparams("parallel", "parallel", "parallel"),
        name="gdn_prep",
    )(proj, proj, proj, conv_w)


def _gdn_gates_kernel(ab_ref, neg_a_ref, dt_ref, is_a_ref, is_bwd_ref, o_ref):
    x = ab_ref[0]
    tl = x.shape[0]
    z = x + dt_ref[...]
    g = neg_a_ref[...] * (jnp.maximum(z, 0.0) + jnp.log(1.0 + jnp.exp(-jnp.abs(z))))
    beta = 1.0 / (1.0 + jnp.exp(-x))
    pos = lax.broadcasted_iota(jnp.int32, x.shape, 0) % GDN_CHUNK
    fwd = g
    bwd = g
    s = 1
    while s < GDN_CHUNK:
        fwd = fwd + jnp.where(pos >= s, pltpu.roll(fwd, s, axis=0), 0.0)
        bwd = bwd + jnp.where(pos < GDN_CHUNK - s, pltpu.roll(bwd, tl - s, axis=0), 0.0)
        s *= 2
    gc = jnp.where(is_bwd_ref[...] > 0, bwd, fwd)
    o_ref[0] = jnp.where(is_a_ref[...] > 0, gc, beta)


def _gdn_gates(ab, a_log, dt_bias):
    bsz, l, _ = ab.shape
    nh = GDN_HEADS
    neg_a = jnp.zeros((2, 2, nh), F32).at[:, 0].set(-jnp.exp(a_log.astype(F32)))
    dt = jnp.zeros((2, 2, nh), F32).at[:, 0].set(dt_bias.astype(F32))
    is_a = jnp.zeros((2, 2, nh), F32).at[:, 0].set(1.0)
    is_bwd = jnp.zeros((2, 2, nh), F32).at[1].set(1.0)
    lanes = lambda v: jnp.pad(v.reshape(1, 4 * nh), ((0, 0), (0, LANES - 4 * nh)))
    tl = SEQ_TILE
    vec = pl.BlockSpec((1, LANES), lambda b, t: (0, 0))
    return pl.pallas_call(
        _gdn_gates_kernel,
        grid=(bsz, l // tl),
        in_specs=[pl.BlockSpec((1, tl, LANES), lambda b, t: (b, t, 0)), vec, vec, vec, vec],
        out_specs=pl.BlockSpec((1, tl, LANES), lambda b, t: (b, t, 0)),
        out_shape=jax.ShapeDtypeStruct((bsz, l, LANES), F32),
        compiler_params=_cparams("parallel", "parallel"),
        name="gdn_gates",
    )(ab, lanes(neg_a), lanes(dt), lanes(is_a), lanes(is_bwd))


def _gdn_local_kernel(q_ref, k_ref, v_ref, gcol_ref, grow_ref, qo_f_ref, mn_f_ref, qo_b_ref, mn_b_ref):
    cs = GDN_CHUNK
    hd = HEAD_DIM
    tl = SEQ_TILE
    n_sub = q_ref.shape[1] // tl
    ii = lax.broadcasted_iota(jnp.int32, (tl, tl), 0)
    jj = lax.broadcasted_iota(jnp.int32, (tl, tl), 1)
    same = (ii // cs) == (jj // cs)
    outs = ((qo_f_ref, mn_f_ref), (qo_b_ref, mn_b_ref))
    jobs = [(sub, d) for sub in range(n_sub) for d in range(2)]
    rows = lambda sub: slice(sub * tl, (sub + 1) * tl)
    qk = [_dot_nt(q_ref[0, rows(s), :], k_ref[0, rows(s), :]) for s in range(n_sub)]
    kk = [_dot_nt(k_ref[0, rows(s), :], k_ref[0, rows(s), :]) for s in range(n_sub)]
    gc = [gcol_ref[0, 0, rows(s), 2 * d:2 * d + 1] for s, d in jobs]
    beta = [gcol_ref[0, 0, rows(s), 2 * d + 1:2 * d + 2] for s, d in jobs]
    gr = [grow_ref[0, 0, 2 * d:2 * d + 1, rows(s)] for s, d in jobs]
    incl = [jnp.logical_and(same, (ii >= jj) if d == 0 else (ii <= jj)) for s, d in jobs]
    strict = [jnp.logical_and(same, (ii > jj) if d == 0 else (ii < jj)) for s, d in jobs]
    decay = [jnp.where(m, jnp.exp(jnp.where(m, a - b, 0.0)), 0.0) for m, a, b in zip(incl, gc, gr)]
    pw = [jnp.where(m, kk[s] * dc, 0.0) * (-bt) for (s, d), m, dc, bt in zip(jobs, strict, decay, beta)]
    acc = list(pw)
    step = 2
    while step < cs:
        p16 = [p.astype(BF16) for p in pw]
        pw = [_dot(p, p) for p in p16]
        acc = [a + p + _dot(a.astype(BF16), p.astype(BF16)) for a, p in zip(acc, pw)]
        step *= 2
    e_gc = [jnp.exp(g) for g in gc]
    rhs = [jnp.concatenate([v_ref[0, rows(s), :].astype(F32) * bt, k_ref[0, rows(s), :].astype(F32) * (bt * eg)], axis=-1)
           for (s, d), bt, eg in zip(jobs, beta, e_gc)]
    uw16 = [(r + _dot(a.astype(BF16), r.astype(BF16))).astype(BF16) for r, a in zip(rhs, acc)]
    au_aw = [_dot((qk[s] * dc).astype(BF16), x) for (s, d), dc, x in zip(jobs, decay, uw16)]
    for (s, d), eg, aa in zip(jobs, e_gc, au_aw):
        q = q_ref[0, rows(s), :].astype(F32)
        outs[d][0][0, 0, rows(s), :] = jnp.concatenate([q * eg - aa[:, hd:], aa[:, :hd]], axis=-1).astype(BF16)
    for (s, d), g, r, x in zip(jobs, gc, gr, uw16):
        edge = jnp.logical_and(same, (jj % cs) == (cs - 1 if d == 0 else 0))
        g_last = jnp.sum(jnp.where(edge, r, 0.0), axis=-1, keepdims=True)
        kd16 = (k_ref[0, rows(s), :].astype(F32) * jnp.exp(g_last - g)).astype(BF16)
        for c in range(tl // cs):
            rs = slice(c * cs, (c + 1) * cs)
            ku_kw = _dot_tn(kd16[rs], x[rs])
            m0 = (s * (tl // cs) + c) * hd
            outs[d][1][0, 0, m0:m0 + hd, :] = jnp.concatenate([-ku_kw[:, hd:], ku_kw[:, :hd]], axis=-1).astype(BF16)


def _gdn_local(qkv, gcol, grow):
    bsz, l, _ = qkv.shape
    nh = GDN_HEADS
    hd = HEAD_DIM
    tl = min(l, GDN_LOCAL_ROWS)
    per_tile = tl // GDN_CHUNK * hd
    col = lambda off: pl.BlockSpec((1, tl, hd), lambda b, h, t: (b, t, off + h))
    qo = pl.BlockSpec((1, 1, tl, 2 * hd), lambda b, h, t: (b, h, t, 0))
    mn = pl.BlockSpec((1, 1, per_tile, 2 * hd), lambda b, h, t: (b, h, t, 0))
    s_qo = jax.ShapeDtypeStruct((bsz, nh, l, 2 * hd), BF16)
    s_mn = jax.ShapeDtypeStruct((bsz, nh, l // GDN_CHUNK * hd, 2 * hd), BF16)
    return pl.pallas_call(
        _gdn_local_kernel,
        grid=(bsz, nh, l // tl),
        in_specs=[col(0), col(nh), col(2 * nh),
                  pl.BlockSpec((1, 1, tl, 4), lambda b, h, t: (b, h, t, 0)),
                  pl.BlockSpec((1, 1, 4, tl), lambda b, h, t: (b, h, 0, t))],
        out_specs=[qo, mn, qo, mn],
        out_shape=[s_qo, s_mn, s_qo, s_mn],
        compiler_params=_cparams("parallel", "parallel", "parallel"),
        name="gdn_local",
    )(qkv, qkv, qkv, gcol, grow)


GDN_SEQ_REFS = 6


def _gdn_scan_kernel(ctx_out, *refs):
    lat = refs[0:GDN_SEQ_REFS]
    ctx = refs[GDN_SEQ_REFS:2 * GDN_SEQ_REFS]
    nw_ref = refs[2 * GDN_SEQ_REFS]
    if ctx_out:
        o_ref, oc_ref, of_ref, ob_ref, ocf_ref, ocb_ref = refs[2 * GDN_SEQ_REFS + 1:]
    else:
        o_ref, of_ref, ob_ref = refs[2 * GDN_SEQ_REFS + 1:]
        oc_ref = ocf_ref = ocb_ref = None
    cs = GDN_CHUNK
    hd = HEAD_DIM

    def scan(seq_refs, out_f, out_b, state):
        qo_f, mn_f, qo_b, mn_b, gcol, _ = seq_refs
        n = qo_f.shape[2] // cs

        def one(qo, mn, c, g_row, g_lane, s, out):
            s16 = s.astype(BF16)
            r0 = pl.multiple_of(c * cs, cs)
            m0 = pl.multiple_of(c * hd, hd)
            if out is not None:
                out[pl.ds(r0, cs), :] = (_dot(qo[0, 0, pl.ds(r0, cs), 0:hd], s16)
                                         + qo[0, 0, pl.ds(r0, cs), hd:2 * hd].astype(F32))
            e_last = jnp.exp(gcol[0, 0, pl.ds(g_row, 1), g_lane:g_lane + 1])
            return (s * e_last + _dot(mn[0, 0, pl.ds(m0, hd), 0:hd], s16)
                    + mn[0, 0, pl.ds(m0, hd), hd:2 * hd].astype(F32))

        def step(i, st):
            s_f, s_b = st
            cb = n - 1 - i
            s_f = one(qo_f, mn_f, i, i * cs + cs - 1, 0, s_f, out_f)
            s_b = one(qo_b, mn_b, cb, cb * cs, 2, s_b, out_b)
            return s_f, s_b

        return lax.fori_loop(0, n, step, state)

    def finish(seq_refs, out_f, out_b, dst):
        z = seq_refs[GDN_SEQ_REFS - 1][0].astype(F32)
        o = _rms(out_f[...] + out_b[...], nw_ref[...])
        dst[0] = (o * _silu(z)).astype(dst.dtype)

    zero = jnp.zeros((hd, hd), F32)
    state = scan(ctx, ocf_ref, ocb_ref, (zero, zero))
    scan(lat, of_ref, ob_ref, state)
    finish(lat, of_ref, ob_ref, o_ref)
    if ctx_out:
        finish(ctx, ocf_ref, ocb_ref, oc_ref)


def _gdn_scan(loc_l, gcol_l, proj_l, loc_c, gcol_c, proj_c, norm_w, ctx_out):
    bsz, nh, seq, _ = loc_l[0].shape
    n_ctx = loc_c[0].shape[2]
    hd = HEAD_DIM
    z_col = Z_COL // hd

    def specs(l):
        full = lambda r, w: pl.BlockSpec((1, 1, r, w), lambda b, h: (b, h, 0, 0))
        one_dir = [full(l, 2 * hd), full(l // GDN_CHUNK * hd, 2 * hd)]
        return one_dir * 2 + [full(l, 4), pl.BlockSpec((1, l, hd), lambda b, h: (b, 0, z_col + h))]

    out_shape = [jax.ShapeDtypeStruct((bsz, seq, nh * hd), BF16)]
    out_specs = [pl.BlockSpec((1, seq, hd), lambda b, h: (b, 0, h))]
    scratch = [pltpu.VMEM((seq, hd), F32), pltpu.VMEM((seq, hd), F32)]
    if ctx_out:
        out_shape.append(jax.ShapeDtypeStruct((bsz, n_ctx, nh * hd), BF16))
        out_specs.append(pl.BlockSpec((1, n_ctx, hd), lambda b, h: (b, 0, h)))
        scratch += [pltpu.VMEM((n_ctx, hd), F32), pltpu.VMEM((n_ctx, hd), F32)]
    res = pl.pallas_call(
        functools.partial(_gdn_scan_kernel, ctx_out),
        grid=(bsz, nh),
        in_specs=specs(seq) + specs(n_ctx) + [pl.BlockSpec((1, hd), lambda b, h: (0, 0))],
        out_specs=out_specs,
        out_shape=out_shape,
        scratch_shapes=scratch,
        compiler_params=_cparams("parallel", "parallel"),
        name="gdn_scan",
    )(*loc_l, gcol_l, proj_l, *loc_c, gcol_c, proj_c, norm_w.reshape(1, hd))
    return (res[0], res[1]) if ctx_out else (res[0], None)


def _gate_layouts(gb):
    bsz, l, _ = gb.shape
    nh = GDN_HEADS
    t = gb[:, :, :4 * nh].reshape(bsz, l, 2, 2, nh)
    gcol = jnp.transpose(t, (0, 4, 1, 2, 3)).reshape(bsz, nh, l, 4)
    grow = jnp.transpose(t, (0, 4, 2, 3, 1)).reshape(bsz, nh, 4, l)
    return gcol, grow


NA_WIDTH = NA_HEADS * HEAD_DIM
POOL_WIDTH = LANES * len(POOL_WINDOWS)
GDN_WIDTH = GDN_HEADS * HEAD_DIM
POOL_COL = 0
NA_COL = POOL_WIDTH
GDN_COL = NA_COL + 3 * NA_WIDTH
Z_COL = GDN_COL + 3 * GDN_WIDTH
AB_COL = Z_COL + GDN_WIDTH


def _reorder_in_proj(w):
    na = 3 * NA_WIDTH
    return jnp.concatenate([w[:, na:na + POOL_WIDTH], w[:, :na], w[:, na + POOL_WIDTH:]], axis=1)


def _token_mixers(proj_l, ab_l, proj_c, ab_c, lw, ctx_out):
    y_na_l, y_na_c = _na(proj_l, proj_c, lw["na_bias"], lw["na_q_norm"], lw["na_k_norm"], ctx_out)
    y_pool_l = _pool(proj_l, lw["bands"], lw["pool_w"], lw["pool_scale"])
    y_pool_c = _pool(proj_c, lw["bands"], lw["pool_w"], lw["pool_scale"]) if ctx_out else None

    def local(proj, ab):
        gb = _gdn_gates(ab, lw["gdn_a_log"], lw["gdn_dt_bias"])
        gcol, grow = _gate_layouts(gb)
        return _gdn_local(_gdn_prep(proj, lw["gdn_conv"]), gcol, grow), gcol

    loc_l, gcol_l = local(proj_l, ab_l)
    loc_c, gcol_c = local(proj_c, ab_c)
    y_gdn_l, y_gdn_c = _gdn_scan(loc_l, gcol_l, proj_l, loc_c, gcol_c, proj_c, lw["gdn_norm"], ctx_out)
    return (y_na_l, y_pool_l, y_gdn_l), (y_na_c, y_pool_c, y_gdn_c)


def kernel(x, c, ctx, c_ctx, w_mod, b_mod, w_in, w_out, na_q_norm, na_k_norm, na_rpb, pool_w, pool_scale,
           gdn_conv, gdn_a_log, gdn_dt_bias, gdn_norm, ffn_w_gate, ffn_w_up, ffn_w_down,
           moe_router, moe_w_gate, moe_w_up, moe_w_down):
    bsz, seq, d = x.shape
    n_ctx = ctx.shape[1]
    depth = w_mod.shape[0]
    assert seq % (GRID_W * NA_Q_ROWS * NA_GROUPS_PER_STEP) == 0 and seq // GRID_W >= NA_K_ROWS
    assert seq % SEQ_TILE == 0 and n_ctx % SEQ_TILE == 0 and w_in.shape[2] == AB_COL + 4 * GDN_HEADS

    mod_rows = 16
    c_rows = jnp.zeros((mod_rows, d), F32).at[:bsz].set(c).at[bsz].set(c_ctx)
    mods = _modulation(c_rows, w_mod, b_mod).reshape(depth, mod_rows, 6, d)
    bands = _pool_bands()

    xl = x
    xc = ctx.reshape(1, bsz * n_ctx, d)
    for i in range(depth):
        last = i == depth - 1
        mod_l = mods[i, :bsz]
        mod_c = mods[i, bsz:bsz + 1]
        w_main = _reorder_in_proj(w_in[i, :, :AB_COL]).astype(BF16)
        w_ab = jnp.pad(w_in[i, :, AB_COL:], ((0, 0), (0, LANES - 4 * GDN_HEADS))).astype(BF16)
        wo = w_out[i].astype(BF16)
        ws = (wo[:NA_WIDTH], wo[NA_WIDTH:NA_WIDTH + POOL_WIDTH], wo[NA_WIDTH + POOL_WIDTH:])
        lw = dict(na_bias=_na_bias_table(na_rpb[i], seq // GRID_W), na_q_norm=na_q_norm[i], na_k_norm=na_k_norm[i],
                  bands=bands, pool_w=pool_w[i].astype(BF16), pool_scale=pool_scale[i], gdn_conv=gdn_conv[i],
                  gdn_a_log=gdn_a_log[i], gdn_dt_bias=gdn_dt_bias[i], gdn_norm=gdn_norm[i])

        h_l = _norm_mod(xl, mod_l, 0).reshape(bsz * seq, d)
        h_c = _norm_mod(xc, mod_c, 0).reshape(bsz * n_ctx, d)
        half = AB_COL // 2
        proj_l = _mm(h_l, w_main, BF16, TOK_TILE, half).reshape(bsz, seq, AB_COL)
        proj_c = _mm(h_c, w_main, BF16, TOK_TILE, half).reshape(bsz, n_ctx, AB_COL)
        ab_l = _mm(h_l, w_ab, F32, 2048, LANES).reshape(bsz, seq, LANES)
        ab_c = _mm(h_c, w_ab, F32, 2048, LANES).reshape(bsz, n_ctx, LANES)
        ys_l, ys_c = _token_mixers(proj_l, ab_l, proj_c, ab_c, lw, not last)
        xl = _out_proj(xl, mod_l, 2, ys_l, ws)
        if not last:
            xc = _out_proj(xc, mod_c, 2, [y.reshape(1, bsz * n_ctx, -1) for y in ys_c], ws)

        streams = [(xl, mod_l)] if last else [(xl, mod_l), (xc, mod_c)]
        outs = []
        j = i // 2
        for xs, mod in streams:
            if i % 2 == 0:
                h = _norm_mod(xs, mod, 3)
                outs.append(_ffn(xs, mod, 5, h, ffn_w_gate[j].astype(BF16), ffn_w_up[j].astype(BF16),
                                 ffn_w_down[j].astype(BF16)))
            else:
                h, h_packed = _norm_mod(xs, mod, 3, packed=True)
                rw = jnp.pad(moe_router[j], ((0, 0), (0, LANES - N_EXPERTS))).astype(BF16)
                outs.append(_moe(xs, mod, 5, h, h_packed, rw, moe_w_gate[j].astype(BF16), moe_w_up[j].astype(BF16),
                                 moe_w_down[j].astype(BF16)))
        xl = outs[0]
        if not last:
            xc = outs[1]
    return xl
```
